```python
import jax, jax.numpy as jnp
from jax import lax
import numpy as np

D_MODEL = 1024
BATCH = 4
SEQ = 8192
DEPTH = 2

GRID_W = 64
CTX_LEN = 256
N_EVEN = (DEPTH + 1) // 2
N_ODD = DEPTH // 2
N_MOD = 6
EPS = 1e-6

POOL_WINDOWS = (2, 4, 8, 16)
N_POOL = 4
POOL_GROUP = 3 * D_MODEL // 16
POOL_WIDTH = N_POOL * POOL_GROUP
FOURIER_WIDTH = D_MODEL // 4
AB_WIDTH = POOL_WIDTH + FOURIER_WIDTH

HEAD_DIM = 64
N_Q_HEADS = D_MODEL // 128
N_KV_HEADS = N_Q_HEADS // 4
GQA_GROUP = N_Q_HEADS // N_KV_HEADS
WINDOW = 128
BLOCK = 128
ROPE_THETA = 10000.0
RWKV_HEADS = D_MODEL // 128
RWKV_WIDTH = RWKV_HEADS * HEAD_DIM
DECAY_LORA = 64
AAA_LORA = 64
GATE_LORA = 128
GN_EPS = 64e-5
ATT_WIDTH = N_Q_HEADS * HEAD_DIM
KV_WIDTH = N_KV_HEADS * HEAD_DIM
CD_SIZES = (ATT_WIDTH, KV_WIDTH, KV_WIDTH, 3 * RWKV_WIDTH, DECAY_LORA, AAA_LORA, GATE_LORA)
CD_WIDTH = 2560
CD_OUT = ATT_WIDTH + RWKV_WIDTH

D_FF = 2816

kernel_name = 'hybrid_pool_fourier_swa_rwkv7_dit'


def _rmsnorm(x, g):
    xf = x.astype(jnp.float32)
    y = xf * lax.rsqrt(jnp.mean(xf * xf, axis=-1, keepdims=True) + EPS)
    return (y * g.astype(jnp.float32)).astype(x.dtype)


def _modulate(x, shift, scale):
    return x * (1 + scale) + shift


def _dwconv3(x, w):
    xp = jnp.pad(x, ((0, 0), (1, 1), (0, 0)))
    return xp[:, :-2] * w[0] + xp[:, 1:-1] * w[1] + xp[:, 2:] * w[2]


def _split_sizes(t, sizes):
    out, start = [], 0
    for s in sizes:
        out.append(t[..., start:start + s])
        start += s
    return out


def _pool_minus_self(x, window):
    T = x.shape[1]
    left = window // 2
    right = window - 1 - left
    xf = x.astype(jnp.float32)
    cs = jnp.pad(jnp.cumsum(xf, axis=1), ((0, 0), (1, 0), (0, 0)))
    t = jnp.arange(T)
    lo = jnp.maximum(t - left, 0)
    hi = jnp.minimum(t + right + 1, T)
    mean = (cs[:, hi] - cs[:, lo]) / (hi - lo).astype(jnp.float32)[None, :, None]
    return (mean - xf).astype(x.dtype)


def _ab_mixer(u, w_in, pool_w, pool_scale, fourier_w, w_out):
    B, T, _ = u.shape
    z = u @ w_in
    za, zb = z[..., :POOL_WIDTH], z[..., POOL_WIDTH:]
    groups = jnp.split(za, N_POOL, axis=-1)
    pooled = jnp.stack([_pool_minus_self(g, w) for g, w in zip(groups, POOL_WINDOWS)], axis=2)
    ya = jnp.einsum('btgc,gcd->btgd', pooled, pool_w).reshape(B, T, POOL_WIDTH) * pool_scale
    spec = jnp.fft.fft2(zb.astype(jnp.float32), axes=(1, 2), norm='ortho').real
    yb = spec.astype(z.dtype) @ fourier_w
    return jnp.concatenate([ya, yb], axis=-1) @ w_out


def _axial_rope(x, pos_row, pos_col):
    half = HEAD_DIM // 2
    inv = ROPE_THETA ** (-jnp.arange(0, half, 2, dtype=jnp.float32) / half)
    xf = x.astype(jnp.float32)

    def rot(xs, pos):
        ang = pos.astype(jnp.float32)[:, None] * inv[None, :]
        cos = jnp.cos(ang)[None, :, None, :]
        sin = jnp.sin(ang)[None, :, None, :]
        x1, x2 = xs[..., :half // 2], xs[..., half // 2:]
        return jnp.concatenate([x1 * cos - x2 * sin, x2 * cos + x1 * sin], axis=-1)

    out = jnp.concatenate([rot(xf[..., :half], pos_row), rot(xf[..., half:], pos_col)], axis=-1)
    return out.astype(x.dtype)


def _sink_attend(q, ks, vs, masks, sink):
    B, Q, HK, G, _ = q.shape
    scale = HEAD_DIM ** -0.5
    logits = []
    for kt, m in zip(ks, masks):
        s = jnp.einsum('bqhgd,bkhd->bhgqk', q, kt).astype(jnp.float32) * scale
        if m is not None:
            s = jnp.where(m, s, -jnp.inf)
        logits.append(s)
    logits.append(jnp.broadcast_to(sink.reshape(HK, G, 1, 1).astype(jnp.float32), (B, HK, G, Q, 1)))
    p = jax.nn.softmax(jnp.concatenate(logits, axis=-1), axis=-1)
    outs, off = [], 0
    for vt in vs:
        K = vt.shape[1]
        outs.append(jnp.einsum('bhgqk,bkhd->bqhgd', p[..., off:off + K].astype(vt.dtype), vt))
        off += K
    out = outs[0]
    for o in outs[1:]:
        out = out + o
    return out


def _window_attention(q, k, v, kc, vc, sink):
    B, T = q.shape[:2]
    nb = T // BLOCK
    pad = ((0, 0), (BLOCK, BLOCK), (0, 0), (0, 0))
    kp, vp = jnp.pad(k, pad), jnp.pad(v, pad)
    qg = q.reshape(B, T, N_KV_HEADS, GQA_GROUP, HEAD_DIM)
    offs_q = jnp.arange(BLOCK)
    offs_k = jnp.arange(3 * BLOCK)

    def one_block(i):
        start = i * BLOCK
        q_i = lax.dynamic_slice_in_dim(qg, start, BLOCK, axis=1)
        k_i = lax.dynamic_slice_in_dim(kp, start, 3 * BLOCK, axis=1)
        v_i = lax.dynamic_slice_in_dim(vp, start, 3 * BLOCK, axis=1)
        qpos = start + offs_q
        kpos = start - BLOCK + offs_k
        m = (jnp.abs(kpos[None, :] - qpos[:, None]) <= WINDOW) & (kpos >= 0)[None, :] & (kpos < T)[None, :]
        return _sink_attend(q_i, [k_i, kc], [v_i, vc], [m, None], sink)

    o = lax.map(one_block, jnp.arange(nb))
    return jnp.moveaxis(o, 0, 1).reshape(B, T, ATT_WIDTH)


def _rwkv_prepare(rkv, wl, al, conv_w, w0, w2, a0, a2, k_k, k_a):
    B, T, _ = rkv.shape
    f32 = jnp.float32
    heads = lambda t: t.reshape(B, T, RWKV_HEADS, HEAD_DIM)
    r, k, v = jnp.split(_dwconv3(rkv, conv_w).astype(f32), 3, axis=-1)
    kk = heads(k * k_k)
    kk = kk * lax.rsqrt(jnp.sum(kk * kk, axis=-1, keepdims=True) + EPS)
    dirs = []
    for d in range(2):
        w_log = -jax.nn.softplus(-(w0[d] + jnp.tanh(wl.astype(f32)) @ w2[d])) - 0.5
        decay = jnp.exp(-jnp.exp(w_log))
        a = jax.nn.sigmoid(a0[d] + al.astype(f32) @ a2[d])
        kmod = k * (1 + (a - 1) * k_a)
        dirs.append((heads(decay), heads(a), heads(kmod)))
    return heads(r), heads(v), kk, dirs


def _rwkv_scan(S0, decay, kk, a, k, v, r, reverse):
    to_time = lambda t: None if t is None else jnp.moveaxis(t, 1, 0)

    def step(S, xs):
        w_t, kk_t, a_t, k_t, v_t, r_t = xs
        sa = jnp.einsum('bhvk,bhk->bhv', S, kk_t)
        S = S * w_t[:, :, None, :] - sa[..., None] * (kk_t * a_t)[:, :, None, :] + v_t[..., None] * k_t[:, :, None, :]
        y = None if r_t is None else jnp.einsum('bhvk,bhk->bhv', S, r_t)
        return S, y

    xs = tuple(to_time(t) for t in (decay, kk, a, k, v, r))
    S, ys = lax.scan(step, S0, xs, reverse=reverse)
    return S, (None if ys is None else jnp.moveaxis(ys, 0, 1))


def _rwkv_out(y, r, v, dirs, gl, r_k, lnx_w, lnx_b, g_up, dtype):
    B, T = y.shape[:2]
    mu = jnp.mean(y, axis=-1, keepdims=True)
    var = jnp.mean(jnp.square(y - mu), axis=-1, keepdims=True)
    yn = ((y - mu) * lax.rsqrt(var + GN_EPS)).reshape(B, T, RWKV_WIDTH) * lnx_w + lnx_b
    bonus = (jnp.sum(r * dirs[0][2] * r_k, axis=-1, keepdims=True)
             + jnp.sum(r * dirs[1][2] * r_k, axis=-1, keepdims=True)) * v
    gate = jax.nn.sigmoid(gl) @ g_up
    return ((yn + bonus.reshape(B, T, RWKV_WIDTH)) * gate).astype(dtype)


def _cd_mixer(u, uc, pos_row, pos_col, ctx_out, w_in, q_norm, k_norm, sink, conv_w,
              w0, w2, a0, a2, k_k, k_a, r_k, lnx_w, lnx_b, g_up, w_out):
    B, T, _ = u.shape
    C = uc.shape[1]
    q, k, v, rkv, wl, al, gl = _split_sizes(u @ w_in, CD_SIZES)
    q_c, k_c, v_c, rkv_c, wl_c, al_c, gl_c = _split_sizes(uc @ w_in, CD_SIZES)
    q = _axial_rope(_rmsnorm(q.reshape(B, T, N_Q_HEADS, HEAD_DIM), q_norm), pos_row, pos_col)
    k = _axial_rope(_rmsnorm(k.reshape(B, T, N_KV_HEADS, HEAD_DIM), k_norm), pos_row, pos_col)
    v = v.reshape(B, T, N_KV_HEADS, HEAD_DIM)
    k_c = _rmsnorm(k_c.reshape(B, C, N_KV_HEADS, HEAD_DIM), k_norm)
    v_c = v_c.reshape(B, C, N_KV_HEADS, HEAD_DIM)
    att = _window_attention(q, k, v, k_c, v_c, sink)
    rw = (conv_w, w0, w2, a0, a2, k_k, k_a)
    rr_c, rv_c, rkk_c, dirs_c = _rwkv_prepare(rkv_c, wl_c, al_c, *rw)
    rr, rv, rkk, dirs = _rwkv_prepare(rkv, wl, al, *rw)
    S0 = jnp.zeros((B, RWKV_HEADS, HEAD_DIM, HEAD_DIM), jnp.float32)
    ys, ys_c = [], []
    for d, rev in enumerate((False, True)):
        dec_c, a_c, km_c = dirs_c[d]
        S_ctx, yc_d = _rwkv_scan(S0, dec_c, rkk_c, a_c, km_c, rv_c, rr_c if ctx_out else None, rev)
        dec, a, km = dirs[d]
        _, y_d = _rwkv_scan(S_ctx, dec, rkk, a, km, rv, rr, rev)
        ys.append(y_d)
        ys_c.append(yc_d)
    mix = _rwkv_out(ys[0] + ys[1], rr, rv, dirs, gl, r_k, lnx_w, lnx_b, g_up, u.dtype)
    out = jnp.concatenate([att, mix], axis=-1) @ w_out
    if not ctx_out:
        return out, None
    q_c = _rmsnorm(q_c.reshape(B, C, N_KV_HEADS, GQA_GROUP, HEAD_DIM), q_norm)
    att_c = _sink_attend(q_c, [k_c], [v_c], [None], sink).reshape(B, C, ATT_WIDTH)
    mix_c = _rwkv_out(ys_c[0] + ys_c[1], rr_c, rv_c, dirs_c, gl_c, r_k, lnx_w, lnx_b, g_up, uc.dtype)
    out_c = jnp.concatenate([att_c, mix_c], axis=-1) @ w_out
    return out, out_c


def _conv_ffn(u, w_up, conv_w, w_down):
    gate, val = jnp.split(_dwconv3(u @ w_up, conv_w), 2, axis=-1)
    return (jax.nn.silu(gate) * val) @ w_down


def setup_inputs(seed: int = 0) -> dict:
    key = jax.random.key(seed)
    keys = iter(jax.random.split(key, 48))

    def nrm(shape, std):
        return jax.random.normal(next(keys), shape, jnp.float32) * std

    D = D_MODEL
    return {
        'x': nrm((BATCH, SEQ, D), 1.0),
        'c': nrm((BATCH, D), 1.0),
        'ctx': nrm((BATCH, CTX_LEN, D), 1.0),
        'c_ctx': nrm((D,), 1.0),
        'ada_w': nrm((DEPTH, D, N_MOD * D), 0.5 * D ** -0.5),
        'ada_b': nrm((DEPTH, N_MOD * D), 0.02),
        'norm1': 1.0 + nrm((DEPTH, D), 0.05),
        'norm2': 1.0 + nrm((DEPTH, D), 0.05),
        'ffn_up': nrm((DEPTH, D, 2 * D_FF), D ** -0.5),
        'ffn_conv': nrm((DEPTH, 3, 2 * D_FF), 3 ** -0.5),
        'ffn_down': nrm((DEPTH, D_FF, D), D_FF ** -0.5),
        'ab_w_in': nrm((N_EVEN, D, AB_WIDTH), D ** -0.5),
        'pool_w': nrm((N_EVEN, N_POOL, POOL_GROUP, POOL_GROUP), POOL_GROUP ** -0.5),
        'pool_scale': 1.0 + nrm((N_EVEN, POOL_WIDTH), 0.1),
        'fourier_w': nrm((N_EVEN, FOURIER_WIDTH, FOURIER_WIDTH), FOURIER_WIDTH ** -0.5),
        'ab_w_out': nrm((N_EVEN, AB_WIDTH, D), AB_WIDTH ** -0.5),
        'cd_w_in': nrm((N_ODD, D, CD_WIDTH), D ** -0.5),
        'q_norm': 1.0 + nrm((N_ODD, HEAD_DIM), 0.05),
        'k_norm': 1.0 + nrm((N_ODD, HEAD_DIM), 0.05),
        'attn_sink': nrm((N_ODD, N_Q_HEADS), 0.5),
        'rwkv_conv': nrm((N_ODD, 3, 3 * RWKV_WIDTH), 3 ** -0.5),
        'rwkv_w0': jax.random.uniform(next(keys), (N_ODD, 2, RWKV_WIDTH), jnp.float32, -6.0, 0.0),
        'rwkv_w2': nrm((N_ODD, 2, DECAY_LORA, RWKV_WIDTH), 0.5 * DECAY_LORA ** -0.5),
        'rwkv_a0': nrm((N_ODD, 2, RWKV_WIDTH), 0.5),
        'rwkv_a2': nrm((N_ODD, 2, AAA_LORA, RWKV_WIDTH), 0.5 * AAA_LORA ** -0.5),
        'rwkv_k_k': 0.85 + nrm((N_ODD, RWKV_WIDTH), 0.05),
        'rwkv_k_a': 1.0 + nrm((N_ODD, RWKV_WIDTH), 0.05),
        'rwkv_r_k': nrm((N_ODD, RWKV_HEADS, HEAD_DIM), 0.1),
        'rwkv_lnx_w': 1.0 + nrm((N_ODD, RWKV_WIDTH), 0.05),
        'rwkv_lnx_b': nrm((N_ODD, RWKV_WIDTH), 0.02),
        'rwkv_g_up': nrm((N_ODD, GATE_LORA, RWKV_WIDTH), GATE_LORA ** -0.5),
        'cd_w_out': nrm((N_ODD, CD_OUT, D), CD_OUT ** -0.5),
    }


def reference(x, c, ctx, c_ctx, ada_w, ada_b, norm1, norm2, ffn_up, ffn_conv, ffn_down,
              ab_w_in, pool_w, pool_scale, fourier_w, ab_w_out,
              cd_w_in, q_norm, k_norm, attn_sink, rwkv_conv, rwkv_w0, rwkv_w2, rwkv_a0, rwkv_a2,
              rwkv_k_k, rwkv_k_a, rwkv_r_k, rwkv_lnx_w, rwkv_lnx_b, rwkv_g_up, cd_w_out):
    L = x.shape[1]
    rows = L // GRID_W
    pos_row = jnp.repeat(jnp.arange(rows, dtype=jnp.int32), GRID_W)
    pos_col = jnp.broadcast_to(jnp.arange(GRID_W, dtype=jnp.int32)[None, :], (rows, GRID_W)).reshape(-1)
    h, hc = x, ctx
    for layer in range(DEPTH):
        last = layer == DEPTH - 1
        even = layer % 2 == 0
        j = layer // 2
        mod = jax.nn.silu(c) @ ada_w[layer] + ada_b[layer]
        sh1, sc1, gt1, sh2, sc2, gt2 = jnp.split(mod[:, None, :], N_MOD, axis=-1)
        u = _modulate(_rmsnorm(h, norm1[layer]), sh1, sc1)
        if (not last) or (not even):
            mod_c = jax.nn.silu(c_ctx) @ ada_w[layer] + ada_b[layer]
            csh1, csc1, cgt1, csh2, csc2, cgt2 = jnp.split(mod_c, N_MOD, axis=-1)
            uc = _modulate(_rmsnorm(hc, norm1[layer]), csh1, csc1)
        if even:
            ab = (ab_w_in[j], pool_w[j], pool_scale[j], fourier_w[j], ab_w_out[j])
            y = _ab_mixer(u, *ab)
            yc = None if last else _ab_mixer(uc, *ab)
        else:
            y, yc = _cd_mixer(u, uc, pos_row, pos_col, not last, cd_w_in[j], q_norm[j], k_norm[j],
                              attn_sink[j], rwkv_conv[j], rwkv_w0[j], rwkv_w2[j], rwkv_a0[j], rwkv_a2[j],
                              rwkv_k_k[j], rwkv_k_a[j], rwkv_r_k[j], rwkv_lnx_w[j], rwkv_lnx_b[j],
                              rwkv_g_up[j], cd_w_out[j])
        ffn = (ffn_up[layer], ffn_conv[layer], ffn_down[layer])
        h = h + gt1 * y
        h = h + gt2 * _conv_ffn(_modulate(_rmsnorm(h, norm2[layer]), sh2, sc2), *ffn)
        if not last:
            hc = hc + cgt1 * yc
            hc = hc + cgt2 * _conv_ffn(_modulate(_rmsnorm(hc, norm2[layer]), csh2, csc2), *ffn)
    return h
```

```python
import functools

import numpy as np
import jax
import jax.numpy as jnp
from jax import lax
from jax.experimental import pallas as pl
from jax.experimental.pallas import tpu as pltpu

F32 = jnp.float32
BF16 = jnp.bfloat16

D = 1024
N_MOD = 6
EPS = 1e-6
GRID_W = 64
POOL_WINDOWS = (2, 4, 8, 16)
POOL_GROUP = 192
POOL_WIDTH = 768
FOURIER_WIDTH = 256
HEAD_DIM = 64
N_Q_HEADS = 8
N_KV_HEADS = 2
GQA_GROUP = 4
ATT_BLOCK = 128
ROPE_THETA = 10000.0
RWKV_HEADS = 8
RWKV_WIDTH = 512
GN_EPS = 64e-5
ATT_WIDTH = 512
KV_WIDTH = 128
D_FF = 2816
CD_SPLITS = (768, 1536, 256)

TM = 256
HALO = 8
FF_CHUNK = 256
FFT_N2 = 64
FFT_LANES = 4096
SCAN_TC = 64
VMEM_LIMIT = 56 * 1024 * 1024


def _cparams(sem):
    return pltpu.CompilerParams(dimension_semantics=sem, vmem_limit_bytes=VMEM_LIMIT)


def _resident(shape):
    nd = len(shape)
    return pl.BlockSpec(shape, lambda *_: (0,) * nd, pipeline_mode=pl.Buffered(1))


def _norm_mod(x, g, shift, scale):
    ms = jnp.mean(x * x, axis=-1, keepdims=True)
    y = x * lax.rsqrt(ms + EPS) * g
    return y * (1.0 + scale) + shift


def _seg_sum(x, ones_bd):
    hi = x.astype(BF16)
    lo = (x - hi.astype(F32)).astype(BF16)
    return (jnp.dot(hi, ones_bd, preferred_element_type=F32)
            + jnp.dot(lo, ones_bd, preferred_element_type=F32))


def _sigmoid(x):
    return 1.0 / (1.0 + jnp.exp(-x))


class _Rows:
    def __init__(self, B, T, C):
        assert T % TM == 0 and C % TM == 0 and (B * T) % C == 0
        assert T & (T - 1) == 0 and C & (C - 1) == 0
        self.B, self.T, self.C = B, T, C
        self.n_lat = B * T
        self.n_tot = B * T + B * C
        self.lat_tiles = self.n_lat // TM
        self.tot_tiles = self.n_tot // TM
        self.tiles_per_seq = T // TM

    def mod_row(self, i):
        return jnp.where(i < self.lat_tiles, i // self.tiles_per_seq, self.B)

    def seq_pos(self, i):
        is_lat = i < self.lat_tiles
        seqlen = jnp.where(is_lat, self.T, self.C)
        row0 = i * TM - jnp.where(is_lat, 0, self.n_lat)
        return lax.rem(row0, seqlen), seqlen

    def halo_specs(self, width, n_rows, col=0):
        blocks = n_rows // HALO
        per = TM // HALO
        prev = pl.BlockSpec((HALO, width), lambda i: (jnp.maximum(i * per - 1, 0), col))
        nxt = pl.BlockSpec((HALO, width), lambda i: (jnp.minimum((i + 1) * per, blocks - 1), col))
        return prev, nxt


def _dwconv3_ext(xe, w, pos0, seqlen):
    n = TM + 2 * HALO
    pos = pos0 + lax.broadcasted_iota(jnp.int32, (TM, 1), 0)
    xm = pltpu.roll(xe, 1, 0)[HALO:HALO + TM]
    xc = xe[HALO:HALO + TM]
    xn = pltpu.roll(xe, n - 1, 0)[HALO:HALO + TM]
    xm = jnp.where(pos == 0, 0.0, xm)
    xn = jnp.where(pos == seqlen - 1, 0.0, xn)
    return xm * w[0:1] + xc * w[1:2] + xn * w[2:3]


def _mod_kernel(cc_ref, w_ref, b_ref, o_ref):
    x = cc_ref[...]
    a = (x * _sigmoid(x)).astype(BF16)
    o_ref[0] = jnp.dot(a, w_ref[0].astype(BF16), preferred_element_type=F32) + b_ref[0]


def _modulation(cc, ada_w, ada_b):
    depth = ada_w.shape[0]
    nb = N_MOD * D // 1024
    return pl.pallas_call(
        _mod_kernel,
        grid=(depth, nb),
        in_specs=[pl.BlockSpec((8, D), lambda l, j: (0, 0)),
                  pl.BlockSpec((1, D, 1024), lambda l, j: (l, 0, j)),
                  pl.BlockSpec((1, 1, 1024), lambda l, j: (l, 0, j))],
        out_specs=pl.BlockSpec((1, 8, 1024), lambda l, j: (l, 0, j)),
        out_shape=jax.ShapeDtypeStruct((depth, 8, N_MOD * D), F32),
        compiler_params=_cparams(("arbitrary", "arbitrary")),
        name="adaln_mod",
    )(cc, ada_w, ada_b.reshape(depth, 1, N_MOD * D))


def _in_kernel(x_ref, g_ref, mod_ref, w_ref, *o_refs, splits):
    m = mod_ref[0]
    u = _norm_mod(x_ref[...], g_ref[...], m[:, 0:D], m[:, D:2 * D]).astype(BF16)
    off = 0
    for o_ref, n in zip(o_refs, splits):
        o_ref[...] = jnp.dot(u, w_ref[:, off:off + n], preferred_element_type=F32)
        off += n


def _in_proj(rows, h, g, mod3, layer, w_bf16, splits, name):
    n_rows = h.shape[0]
    width = w_bf16.shape[1]
    return pl.pallas_call(
        functools.partial(_in_kernel, splits=splits),
        grid=(n_rows // TM,),
        in_specs=[pl.BlockSpec((TM, D), lambda i: (i, 0)),
                  _resident((1, D)),
                  pl.BlockSpec((1, 1, N_MOD * D), lambda i: (layer * 8 + rows.mod_row(i), 0, 0)),
                  _resident((D, width))],
        out_specs=[pl.BlockSpec((TM, n), lambda i: (i, 0)) for n in splits],
        out_shape=[jax.ShapeDtypeStruct((n_rows, n), F32) for n in splits],
        compiler_params=_cparams(("arbitrary",)),
        name=name,
    )(h, g, mod3, w_bf16)


def _fft_tables(T):
    n1 = T // FFT_N2
    c = np.arange(FOURIER_WIDTH)
    ang = 2.0 * np.pi * (np.outer(c, c) % FOURIER_WIDTH) / FOURIER_WIDTH
    fc = np.concatenate([np.cos(ang), -np.sin(ang)], axis=1) / np.sqrt(FOURIER_WIDTH)
    t1 = np.arange(n1)[None, None, :]
    k1 = np.arange(n1)[None, :, None]
    t2 = np.arange(FFT_N2)[:, None, None]
    ang = 2.0 * np.pi * (((FFT_N2 * t1 + t2) * k1) % T) / T
    cm, sm = np.cos(ang) / np.sqrt(n1), np.sin(ang) / np.sqrt(n1)
    m1 = np.concatenate([np.concatenate([cm, sm], axis=2),
                         np.concatenate([-sm, cm], axis=2)], axis=1)
    k2 = np.arange(FFT_N2)
    ang = 2.0 * np.pi * (np.outer(k2, k2) % FFT_N2) / FFT_N2
    cs2 = np.concatenate([np.cos(ang), np.sin(ang)], axis=1) / np.sqrt(FFT_N2)
    return (jnp.asarray(fc, BF16), jnp.asarray(m1, BF16), jnp.asarray(cs2, BF16))


def _ctx_fft_table(C):
    t = np.arange(C)
    ang = 2.0 * np.pi * (np.outer(t, t) % C) / C
    return jnp.asarray(np.concatenate([np.cos(ang), np.sin(ang)], axis=1) / np.sqrt(C), BF16)


def _fft1_kernel(z_ref, fc_ref, m_ref, y_ref):
    a = jnp.dot(z_ref[...].astype(BF16), fc_ref[...], preferred_element_type=F32)
    st = jnp.concatenate([a[:, :FOURIER_WIDTH], a[:, FOURIER_WIDTH:]], axis=0).astype(BF16)
    y_ref[0, 0] = jnp.dot(m_ref[0], st, preferred_element_type=F32)


def _fft2_kernel(yr_ref, yi_ref, cs_ref, o_ref):
    st = jnp.concatenate([yr_ref[0], yi_ref[0]], axis=0).astype(BF16)
    o_ref[0] = jnp.dot(cs_ref[...], st, preferred_element_type=F32)


def _ctx_fft_kernel(z_ref, fc_ref, ct_ref, o_ref):
    a = jnp.dot(z_ref[...].astype(BF16), fc_ref[...], preferred_element_type=F32)
    st = jnp.concatenate([a[:, :FOURIER_WIDTH], a[:, FOURIER_WIDTH:]], axis=0).astype(BF16)
    o_ref[...] = jnp.dot(ct_ref[...], st, preferred_element_type=F32)


def _fourier_spec(rows, z):
    B, T, C = rows.B, rows.T, rows.C
    n1 = T // FFT_N2
    fc, m1, cs2 = _fft_tables(T)
    zv = z.reshape(rows.n_tot // FFT_N2, FFT_N2 * D)
    lane_blocks = D // FOURIER_WIDTH
    ybuf = pl.pallas_call(
        _fft1_kernel,
        grid=(B, FFT_N2),
        in_specs=[pl.BlockSpec((n1, FOURIER_WIDTH), lambda b, t2: (b, lane_blocks * t2 + lane_blocks - 1)),
                  _resident((FOURIER_WIDTH, 2 * FOURIER_WIDTH)),
                  pl.BlockSpec((1, 2 * n1, 2 * n1), lambda b, t2: (t2, 0, 0))],
        out_specs=pl.BlockSpec((1, 1, 2 * n1, FOURIER_WIDTH), lambda b, t2: (b, t2, 0, 0)),
        out_shape=jax.ShapeDtypeStruct((B, FFT_N2, 2 * n1, FOURIER_WIDTH), F32),
        compiler_params=_cparams(("arbitrary", "arbitrary")),
        name="fft_stage1",
    )(zv, fc, m1)
    half = n1 * FOURIER_WIDTH
    lb = min(FFT_LANES, half)
    nblk = half // lb
    yv = ybuf.reshape(B, FFT_N2, 2 * half)
    spec_lat = pl.pallas_call(
        _fft2_kernel,
        grid=(B, nblk),
        in_specs=[pl.BlockSpec((1, FFT_N2, lb), lambda b, j: (b, 0, j)),
                  pl.BlockSpec((1, FFT_N2, lb), lambda b, j: (b, 0, j + nblk)),
                  _resident((FFT_N2, 2 * FFT_N2))],
        out_specs=pl.BlockSpec((1, FFT_N2, lb), lambda b, j: (b, 0, j)),
        out_shape=jax.ShapeDtypeStruct((B, FFT_N2, half), F32),
        compiler_params=_cparams(("arbitrary", "arbitrary")),
        name="fft_stage2",
    )(yv, yv, cs2)
    ct = _ctx_fft_table(C)
    spec_ctx = pl.pallas_call(
        _ctx_fft_kernel,
        grid=(B,),
        in_specs=[pl.BlockSpec((C, FOURIER_WIDTH), lambda b: (rows.n_lat // C + b, lane_blocks - 1)),
                  _resident((FOURIER_WIDTH, 2 * FOURIER_WIDTH)),
                  _resident((C, 2 * C))],
        out_specs=pl.BlockSpec((C, FOURIER_WIDTH), lambda b: (b, 0)),
        out_shape=jax.ShapeDtypeStruct((B * C, FOURIER_WIDTH), F32),
        compiler_params=_cparams(("arbitrary",)),
        name="fft_ctx",
    )(z, fc, ct)
    return jnp.concatenate([spec_lat.reshape(rows.n_lat, FOURIER_WIDTH), spec_ctx], axis=0)


def _ab_out_kernel(z_ref, zp_ref, zn_ref, spec_ref, h_ref, mod_ref, pw_ref, ps_ref, fw_ref, wo_ref,
                   o_ref, *, rows):
    i = pl.program_id(0)
    pos0, seqlen = rows.seq_pos(i)
    n = TM + 2 * HALO
    ze = jnp.concatenate([zp_ref[...], z_ref[...], zn_ref[...]], axis=0)
    pe = pos0 - HALO + lax.broadcasted_iota(jnp.int32, (n, 1), 0)
    ze = jnp.where((pe >= 0) & (pe < seqlen), ze, 0.0)
    sums = {1: ze}
    w = 1
    while w < max(POOL_WINDOWS):
        sums[2 * w] = sums[w] + pltpu.roll(sums[w], n - w, 0)
        w *= 2
    pos = pos0 + lax.broadcasted_iota(jnp.int32, (TM, 1), 0)
    lane = lax.broadcasted_iota(jnp.int32, (1, POOL_WIDTH), 1)
    zc = ze[HALO:HALO + TM]
    mean = jnp.zeros((TM, POOL_WIDTH), F32)
    for g, w in enumerate(POOL_WINDOWS):
        left = w // 2
        right = w - 1 - left
        cw = pltpu.roll(sums[w], left, 0)[HALO:HALO + TM]
        cnt = (jnp.minimum(pos + right + 1, seqlen) - jnp.maximum(pos - left, 0)).astype(F32)
        in_group = (lane >= g * POOL_GROUP) & (lane < (g + 1) * POOL_GROUP)
        mean = jnp.where(in_group, cw / cnt, mean)
    pooled = (mean - zc).astype(BF16)
    ya = jnp.dot(pooled, pw_ref[...], preferred_element_type=F32) * ps_ref[...]
    yb = jnp.dot(spec_ref[...].astype(BF16), fw_ref[...], preferred_element_type=F32)
    y = (jnp.dot(ya.astype(BF16), wo_ref[0:POOL_WIDTH, :], preferred_element_type=F32)
         + jnp.dot(yb.astype(BF16), wo_ref[POOL_WIDTH:, :], preferred_element_type=F32))
    gate = mod_ref[0][:, 2 * D:3 * D]
    o_ref[...] = h_ref[...] + gate * y


def _ab_out(rows, z, spec, h, mod3, layer, pw_bd, pscale, fw, wo):
    n_rows = rows.n_tot
    zprev, znext = rows.halo_specs(POOL_WIDTH, n_rows)
    return pl.pallas_call(
        functools.partial(_ab_out_kernel, rows=rows),
        grid=(n_rows // TM,),
        in_specs=[pl.BlockSpec((TM, POOL_WIDTH), lambda i: (i, 0)), zprev, znext,
                  pl.BlockSpec((TM, FOURIER_WIDTH), lambda i: (i, 0)),
                  pl.BlockSpec((TM, D), lambda i: (i, 0)),
                  pl.BlockSpec((1, 1, N_MOD * D), lambda i: (layer * 8 + rows.mod_row(i), 0, 0)),
                  _resident((POOL_WIDTH, POOL_WIDTH)), _resident((1, POOL_WIDTH)),
                  _resident((FOURIER_WIDTH, FOURIER_WIDTH)), _resident((D, D))],
        out_specs=pl.BlockSpec((TM, D), lambda i: (i, 0)),
        out_shape=jax.ShapeDtypeStruct((n_rows, D), F32),
        compiler_params=_cparams(("arbitrary",)),
        name="ab_out",
    )(z, z, z, spec, h, mod3, pw_bd, pscale, fw, wo)


def _ffn_kernel(h_ref, hp_ref, hn_ref, g_ref, mod_ref, wu_ref, cw_ref, wd_ref, o_ref, *, rows):
    i = pl.program_id(0)
    pos0, seqlen = rows.seq_pos(i)
    m = mod_ref[0]
    he = jnp.concatenate([hp_ref[...], h_ref[...], hn_ref[...]], axis=0)
    u = _norm_mod(he, g_ref[...], m[:, 3 * D:4 * D], m[:, 4 * D:5 * D]).astype(BF16)
    acc = jnp.zeros((TM, D), F32)
    for c in range(D_FF // FF_CHUNK):
        lo = c * FF_CHUNK
        up_g = jnp.dot(u, wu_ref[:, lo:lo + FF_CHUNK], preferred_element_type=F32)
        up_v = jnp.dot(u, wu_ref[:, D_FF + lo:D_FF + lo + FF_CHUNK], preferred_element_type=F32)
        gate = _dwconv3_ext(up_g, cw_ref[:, lo:lo + FF_CHUNK], pos0, seqlen)
        val = _dwconv3_ext(up_v, cw_ref[:, D_FF + lo:D_FF + lo + FF_CHUNK], pos0, seqlen)
        act = (gate * _sigmoid(gate) * val).astype(BF16)
        acc = acc + jnp.dot(act, wd_ref[lo:lo + FF_CHUNK, :], preferred_element_type=F32)
    o_ref[...] = h_ref[...] + m[:, 5 * D:6 * D] * acc


def _ffn(rows, h, g, mod3, layer, wu, cw, wd):
    n_rows = h.shape[0]
    hprev, hnext = rows.halo_specs(D, n_rows)
    return pl.pallas_call(
        functools.partial(_ffn_kernel, rows=rows),
        grid=(n_rows // TM,),
        in_specs=[pl.BlockSpec((TM, D), lambda i: (i, 0)), hprev, hnext,
                  _resident((1, D)),
                  pl.BlockSpec((1, 1, N_MOD * D), lambda i: (layer * 8 + rows.mod_row(i), 0, 0)),
                  _resident((D, 2 * D_FF)), _resident((3, 2 * D_FF)), _resident((D_FF, D))],
        out_specs=pl.BlockSpec((TM, D), lambda i: (i, 0)),
        out_shape=jax.ShapeDtypeStruct((n_rows, D), F32),
        compiler_params=_cparams(("arbitrary",)),
        name="conv_ffn",
    )(h, h, h, g, mod3, wu, cw, wd)


def _rope_tables(T):
    half = HEAD_DIM // 2
    inv = ROPE_THETA ** (-np.arange(0, half, 2, dtype=np.float64) / half)
    t = np.arange(T)
    ang_r = (t // GRID_W)[:, None] * inv[None, :]
    ang_c = (t % GRID_W)[:, None] * inv[None, :]
    cos = np.concatenate([np.cos(ang_r), np.cos(ang_r), np.cos(ang_c), np.cos(ang_c)], axis=1)
    sin = np.concatenate([-np.sin(ang_r), np.sin(ang_r), -np.sin(ang_c), np.sin(ang_c)], axis=1)
    cos = np.concatenate([np.tile(cos, (1, 2)), np.ones((TM, 128))], axis=0)
    sin = np.concatenate([np.tile(sin, (1, 2)), np.zeros((TM, 128))], axis=0)
    return jnp.asarray(cos, F32), jnp.asarray(sin, F32)


def _rope128(x, cos, sin):
    lane = lax.broadcasted_iota(jnp.int32, (1, 128), 1)
    partner = jnp.where((lane & 31) < 16, pltpu.roll(x, 128 - 16, 1), pltpu.roll(x, 16, 1))
    return x * cos + partner * sin


def _qk_kernel(p_ref, cos_ref, sin_ref, qn_ref, kn_ref, ones_ref, q_out, k_out, v_out):
    p = p_ref[...]
    cos, sin = cos_ref[...], sin_ref[...]
    q = p[:, 0:ATT_WIDTH]
    ms = _seg_sum(q * q, ones_ref[...]) * (1.0 / HEAD_DIM)
    q = q * lax.rsqrt(ms + EPS) * qn_ref[...]
    q = jnp.concatenate([_rope128(q[:, 128 * j:128 * (j + 1)], cos, sin) for j in range(4)], axis=1)
    q_out[...] = (q * HEAD_DIM ** -0.5).astype(BF16)
    k = p[:, ATT_WIDTH:ATT_WIDTH + KV_WIDTH]
    ms = _seg_sum(k * k, ones_ref[0:KV_WIDTH, 0:KV_WIDTH]) * (1.0 / HEAD_DIM)
    k = k * lax.rsqrt(ms + EPS) * kn_ref[...]
    k_out[...] = _rope128(k, cos, sin).astype(BF16)
    v_out[...] = p[:, ATT_WIDTH + KV_WIDTH:].astype(BF16)


def _qk_prep(rows, p_att, q_norm, k_norm, ones_bd):
    cos, sin = _rope_tables(rows.T)
    tps = rows.tiles_per_seq
    tab = lambda i: (jnp.where(i < rows.lat_tiles, lax.rem(i, tps), tps), 0)
    qn = jnp.tile(q_norm, N_Q_HEADS).reshape(1, ATT_WIDTH)
    kn = jnp.tile(k_norm, N_KV_HEADS).reshape(1, KV_WIDTH)
    return pl.pallas_call(
        _qk_kernel,
        grid=(rows.tot_tiles,),
        in_specs=[pl.BlockSpec((TM, CD_SPLITS[0]), lambda i: (i, 0)),
                  pl.BlockSpec((TM, 128), tab), pl.BlockSpec((TM, 128), tab),
                  _resident((1, ATT_WIDTH)), _resident((1, KV_WIDTH)),
                  _resident((RWKV_WIDTH, RWKV_WIDTH))],
        out_specs=[pl.BlockSpec((TM, ATT_WIDTH), lambda i: (i, 0)),
                   pl.BlockSpec((TM, KV_WIDTH), lambda i: (i, 0)),
                   pl.BlockSpec((TM, KV_WIDTH), lambda i: (i, 0))],
        out_shape=[jax.ShapeDtypeStruct((rows.n_tot, ATT_WIDTH), BF16),
                   jax.ShapeDtypeStruct((rows.n_tot, KV_WIDTH), BF16),
                   jax.ShapeDtypeStruct((rows.n_tot, KV_WIDTH), BF16)],
        compiler_params=_cparams(("arbitrary",)),
        name="qk_prep",
    )(p_att, cos, sin, qn, kn, ones_bd)


def _attn_kernel(q_ref, kp_ref, kc_ref, kn_ref, kx_ref, vp_ref, vc_ref, vn_ref, vx_ref, sink_ref,
                 o_ref, *, nq, C):
    i = pl.program_id(1)
    Q = ATT_BLOCK
    nk = 3 * Q + C
    k_all = jnp.concatenate([kp_ref[...], kc_ref[...], kn_ref[...], kx_ref[...]], axis=0)
    v_all = jnp.concatenate([vp_ref[...], vc_ref[...], vn_ref[...], vx_ref[...]], axis=0)
    row = lax.broadcasted_iota(jnp.int32, (GQA_GROUP * Q, 1), 0)
    ql = row & (Q - 1)
    grp = row >> 7
    col = lax.broadcasted_iota(jnp.int32, (1, nk), 1)
    rel = col - ql
    band = (rel >= 0) & (rel <= 2 * Q)
    kvalid = ((col >= Q) | (i > 0)) & ((col < 2 * Q) | (i < nq - 1))
    mask = (band & kvalid & (col < 3 * Q)) | (col >= 3 * Q)
    q = q_ref[...]
    outs = []
    for hk in range(N_KV_HEADS):
        q4 = jnp.concatenate(
            [q[:, HEAD_DIM * (GQA_GROUP * hk + g):HEAD_DIM * (GQA_GROUP * hk + g + 1)]
             for g in range(GQA_GROUP)], axis=0)
        kh = k_all[:, HEAD_DIM * hk:HEAD_DIM * (hk + 1)]
        vh = v_all[:, HEAD_DIM * hk:HEAD_DIM * (hk + 1)]
        s = lax.dot_general(q4, kh, (((1,), (1,)), ((), ())), preferred_element_type=F32)
        s = jnp.where(mask, s, -jnp.inf)
        sk = jnp.zeros((GQA_GROUP * Q, 1), F32)
        for g in range(GQA_GROUP):
            h = GQA_GROUP * hk + g
            sk = jnp.where(grp == g, sink_ref[h:h + 1, 0:1], sk)
        m = jnp.maximum(jnp.max(s, axis=1, keepdims=True), sk)
        p = jnp.exp(s - m)
        den = jnp.sum(p, axis=1, keepdims=True) + jnp.exp(sk - m)
        o = jnp.dot(p.astype(BF16), vh, preferred_element_type=F32) / den
        outs += [o[Q * g:Q * (g + 1)] for g in range(GQA_GROUP)]
    o_ref[...] = jnp.concatenate(outs, axis=1).astype(BF16)


def _attention(rows, qr, kr, vr, sink):
    B, T, C = rows.B, rows.T, rows.C
    nq = T // ATT_BLOCK
    blk = lambda off: pl.BlockSpec(
        (ATT_BLOCK, KV_WIDTH), lambda b, i: (b * nq + jnp.clip(i + off, 0, nq - 1), 0))
    ctx = pl.BlockSpec((C, KV_WIDTH), lambda b, i: (rows.n_lat // C + b, 0))
    sink_b = jnp.broadcast_to(sink.reshape(N_Q_HEADS, 1), (N_Q_HEADS, 128))
    return pl.pallas_call(
        functools.partial(_attn_kernel, nq=nq, C=C),
        grid=(B, nq),
        in_specs=[pl.BlockSpec((ATT_BLOCK, ATT_WIDTH), lambda b, i: (b * nq + i, 0)),
                  blk(-1), blk(0), blk(1), ctx, blk(-1), blk(0), blk(1), ctx,
                  _resident((N_Q_HEADS, 128))],
        out_specs=pl.BlockSpec((ATT_BLOCK, ATT_WIDTH), lambda b, i: (b * nq + i, 0)),
        out_shape=jax.ShapeDtypeStruct((rows.n_lat, ATT_WIDTH), BF16),
        compiler_params=_cparams(("arbitrary", "arbitrary")),
        name="window_attention",
    )(qr, kr, kr, kr, kr, vr, vr, vr, vr, sink_b)


def _rwkv_prep_kernel(x_ref, xp_ref, xn_ref, lora_ref, cw_ref, w0_ref, w2_ref, a0_ref, a2_ref,
                      kk_ref, ka_ref, ones_ref,
                      r_out, v_out, kkn_out, w_out0, kka_out0, km_out0, w_out1, kka_out1, km_out1, *, rows):
    i = pl.program_id(0)
    pos0, seqlen = rows.seq_pos(i)
    xe = jnp.concatenate([xp_ref[...], x_ref[...], xn_ref[...]], axis=0)
    rkv = _dwconv3_ext(xe, cw_ref[...], pos0, seqlen)
    W = RWKV_WIDTH
    r, k, v = rkv[:, 0:W], rkv[:, W:2 * W], rkv[:, 2 * W:3 * W]
    kk = k * kk_ref[...]
    kk = kk * lax.rsqrt(_seg_sum(kk * kk, ones_ref[...]) + EPS)
    r_out[...] = r
    v_out[...] = v
    kkn_out[...] = kk
    lora = lora_ref[...]
    wl = jnp.tanh(lora[:, 0:64]).astype(BF16)
    al = lora[:, 64:128].astype(BF16)
    outs = ((w_out0, kka_out0, km_out0), (w_out1, kka_out1, km_out1))
    for d in range(2):
        x = -(w0_ref[d:d + 1, :] + jnp.dot(wl, w2_ref[d], preferred_element_type=F32))
        softplus = jnp.maximum(x, 0.0) + jnp.log(1.0 + jnp.exp(-jnp.abs(x)))
        w_log = -softplus - 0.5
        a = _sigmoid(a0_ref[d:d + 1, :] + jnp.dot(al, a2_ref[d], preferred_element_type=F32))
        w_o, kka_o, km_o = outs[d]
        w_o[...] = jnp.exp(-jnp.exp(w_log))
        kka_o[...] = kk * a
        km_o[...] = k * (1.0 + (a - 1.0) * ka_ref[...])


def _rwkv_prep(rows, p_rkv, p_lora, conv_w, w0, w2, a0, a2, k_k, k_a, ones_bd):
    n_rows = rows.n_tot
    W = RWKV_WIDTH
    xprev, xnext = rows.halo_specs(3 * W, n_rows)
    out_spec = pl.BlockSpec((TM, W), lambda i: (i, 0))
    return pl.pallas_call(
        functools.partial(_rwkv_prep_kernel, rows=rows),
        grid=(rows.tot_tiles,),
        in_specs=[pl.BlockSpec((TM, 3 * W), lambda i: (i, 0)), xprev, xnext,
                  pl.BlockSpec((TM, CD_SPLITS[2]), lambda i: (i, 0)),
                  _resident((3, 3 * W)), _resident((2, W)), _resident((2, 64, W)),
                  _resident((2, W)), _resident((2, 64, W)), _resident((1, W)), _resident((1, W)),
                  _resident((W, W))],
        out_specs=[out_spec] * 9,
        out_shape=[jax.ShapeDtypeStruct((n_rows, W), F32)] * 9,
        compiler_params=_cparams(("arbitrary",)),
        name="rwkv_prep",
    )(p_rkv, p_rkv, p_rkv, p_lora, conv_w, w0, w2, a0, a2, k_k, k_a, ones_bd)


def _scan_kernel(kk_ref, w_ref, kka_ref, km_ref, r_ref, v_ref, y_ref, s_ref):
    K = HEAD_DIM

    @pl.when(pl.program_id(0) == 0)
    def _():
        s_ref[...] = jnp.zeros_like(s_ref)

    def step(t, carry):
        v_t = v_ref[t]
        acc = [jnp.zeros_like(v_t) for _ in range(4)]
        for k in range(K):
            acc[k % 4] = acc[k % 4] + s_ref[k] * kk_ref[t, k:k + 1, :]
        sa = (acc[0] + acc[1]) + (acc[2] + acc[3])
        yacc = [jnp.zeros_like(v_t) for _ in range(4)]
        for k in range(K):
            s_new = (s_ref[k] * w_ref[t, k:k + 1, :]
                     + (v_t * km_ref[t, k:k + 1, :] - sa * kka_ref[t, k:k + 1, :]))
            s_ref[k] = s_new
            yacc[k % 4] = yacc[k % 4] + s_new * r_ref[t, k:k + 1, :]
        y_ref[t] = (yacc[0] + yacc[1]) + (yacc[2] + yacc[3])
        return carry

    lax.fori_loop(0, SCAN_TC, step, 0)


def _to_chain_layout(rows, x0, x1, is_v):
    B, T, C = rows.B, rows.T, rows.C

    def seq(x, rev):
        lat = x[:rows.n_lat].reshape(B, T, RWKV_HEADS, HEAD_DIM)
        ctx = x[rows.n_lat:].reshape(B, C, RWKV_HEADS, HEAD_DIM)
        if rev:
            lat, ctx = lat[:, ::-1], ctx[:, ::-1]
        return jnp.concatenate([ctx, lat], axis=1)

    s = jnp.stack([seq(x0, False), seq(x1, True)], axis=0)
    steps = C + T
    if is_v:
        s = s.reshape(2, B, steps, RWKV_HEADS, 2, HEAD_DIM // 2)
        s = jnp.transpose(s, (2, 5, 0, 1, 3, 4))
        return s.reshape(steps, HEAD_DIM // 2, 128)
    s = jnp.transpose(s, (2, 4, 0, 1, 3))
    s = jnp.broadcast_to(s[..., None], s.shape + (2,))
    return s.reshape(steps, HEAD_DIM, 128)


def _from_chain_layout(rows, y):
    B, T, C = rows.B, rows.T, rows.C
    steps = C + T
    y = y.reshape(steps, HEAD_DIM // 2, 2, B, RWKV_HEADS, 2)
    y = jnp.transpose(y, (2, 3, 0, 4, 5, 1)).reshape(2, B, steps, RWKV_WIDTH)
    y0 = y[0][:, C:].reshape(rows.n_lat, RWKV_WIDTH)
    y1 = y[1][:, C:][:, ::-1].reshape(rows.n_lat, RWKV_WIDTH)
    return y0, y1


def _rwkv_scan(rows, r, v, kk, w0, kka0, km0, w1, kka1, km1):
    assert rows.B * RWKV_HEADS * 4 == 128
    steps = rows.C + rows.T
    args = [_to_chain_layout(rows, kk, kk, False), _to_chain_layout(rows, w0, w1, False),
            _to_chain_layout(rows, kka0, kka1, False), _to_chain_layout(rows, km0, km1, False),
            _to_chain_layout(rows, r, r, False), _to_chain_layout(rows, v, v, True)]
    kspec = pl.BlockSpec((SCAN_TC, HEAD_DIM, 128), lambda c: (c, 0, 0))
    vspec = pl.BlockSpec((SCAN_TC, HEAD_DIM // 2, 128), lambda c: (c, 0, 0))
    y = pl.pallas_call(
        _scan_kernel,
        grid=(steps // SCAN_TC,),
        in_specs=[kspec] * 5 + [vspec],
        out_specs=vspec,
        out_shape=jax.ShapeDtypeStruct((steps, HEAD_DIM // 2, 128), F32),
        scratch_shapes=[pltpu.VMEM((HEAD_DIM, HEAD_DIM // 2, 128), F32)],
        compiler_params=_cparams(("arbitrary",)),
        name="rwkv_scan",
    )(*args)
    return _from_chain_layout(rows, y)


def _cd_out_kernel(y0_ref, y1_ref, r_ref, v_ref, km0_ref, km1_ref, lora_ref, att_ref, h_ref, mod_ref,
                   rk_ref, lw_ref, lb_ref, gu_ref, wo_ref, ones_ref, o_ref):
    ones_bd = ones_ref[...]
    inv = 1.0 / HEAD_DIM
    y = y0_ref[...] + y1_ref[...]
    mu = _seg_sum(y, ones_bd) * inv
    yc = y - mu
    var = _seg_sum(yc * yc, ones_bd) * inv
    yn = yc * lax.rsqrt(var + GN_EPS) * lw_ref[...] + lb_ref[...]
    r = r_ref[...]
    rk = rk_ref[...]
    bonus = (_seg_sum(r * km0_ref[...] * rk, ones_bd) + _seg_sum(r * km1_ref[...] * rk, ones_bd)) * v_ref[...]
    gate = jnp.dot(_sigmoid(lora_ref[...][:, 128:256]).astype(BF16), gu_ref[...], preferred_element_type=F32)
    mix = ((yn + bonus) * gate).astype(BF16)
    out = (jnp.dot(att_ref[...], wo_ref[0:ATT_WIDTH, :], preferred_element_type=F32)
           + jnp.dot(mix, wo_ref[ATT_WIDTH:, :], preferred_element_type=F32))
    o_ref[...] = h_ref[...] + mod_ref[0][:, 2 * D:3 * D] * out


def _cd_out(rows, y0, y1, r, v, km0, km1, p_lora, att, h, mod3, layer, r_k, lnx_w, lnx_b, g_up, wo, ones_bd):
    W = RWKV_WIDTH
    tile = lambda n: pl.BlockSpec((TM, n), lambda i: (i, 0))
    return pl.pallas_call(
        _cd_out_kernel,
        grid=(rows.lat_tiles,),
        in_specs=[tile(W)] * 6 + [tile(CD_SPLITS[2]), tile(ATT_WIDTH), tile(D),
                  pl.BlockSpec((1, 1, N_MOD * D), lambda i: (layer * 8 + rows.mod_row(i), 0, 0)),
                  _resident((1, W)), _resident((1, W)), _resident((1, W)),
                  _resident((128, W)), _resident((D, D)), _resident((W, W))],
        out_specs=tile(D),
        out_shape=jax.ShapeDtypeStruct((rows.n_lat, D), F32),
        compiler_params=_cparams(("arbitrary",)),
        name="cd_out",
    )(y0, y1, r, v, km0, km1, p_lora, att, h, mod3, r_k, lnx_w, lnx_b, g_up, wo, ones_bd)


def _block_diag(blocks):
    n = blocks.shape[0]
    g = blocks.shape[1]
    out = jnp.zeros((n * g, n * g), blocks.dtype)
    for j in range(n):
        out = out.at[j * g:(j + 1) * g, j * g:(j + 1) * g].set(blocks[j])
    return out


def kernel(x, c, ctx, c_ctx, ada_w, ada_b, norm1, norm2, ffn_up, ffn_conv, ffn_down, ab_w_in, pool_w,
           pool_scale, fourier_w, ab_w_out, cd_w_in, q_norm, k_norm, attn_sink, rwkv_conv, rwkv_w0,
           rwkv_w2, rwkv_a0, rwkv_a2, rwkv_k_k, rwkv_k_a, rwkv_r_k, rwkv_lnx_w, rwkv_lnx_b,
           rwkv_g_up, cd_w_out):
    B, T, _ = x.shape
    C = ctx.shape[1]
    depth = ada_w.shape[0]
    assert depth == 2 and B + 1 <= 8
    rows = _Rows(B, T, C)
    W = RWKV_WIDTH

    cc = jnp.zeros((8, D), F32).at[:B].set(c).at[B].set(c_ctx)
    mod3 = _modulation(cc, ada_w, ada_b).reshape(depth * 8, 1, N_MOD * D)
    ones_bd = _block_diag(jnp.ones((RWKV_HEADS, HEAD_DIM, HEAD_DIM), BF16))

    h = jnp.concatenate([x.reshape(B * T, D), ctx.reshape(B * C, D)], axis=0)

    (z,) = _in_proj(rows, h, norm1[0].reshape(1, D), mod3, 0, ab_w_in[0].astype(BF16), (D,), "ab_in")
    spec = _fourier_spec(rows, z)
    h = _ab_out(rows, z, spec, h, mod3, 0, _block_diag(pool_w[0]).astype(BF16),
                pool_scale[0].reshape(1, POOL_WIDTH), fourier_w[0].astype(BF16), ab_w_out[0].astype(BF16))
    h = _ffn(rows, h, norm2[0].reshape(1, D), mod3, 0, ffn_up[0].astype(BF16), ffn_conv[0],
             ffn_down[0].astype(BF16))

    p_att, p_rkv, p_lora = _in_proj(rows, h, norm1[1].reshape(1, D), mod3, 1, cd_w_in[0].astype(BF16),
                                    CD_SPLITS, "cd_in")
    qr, kr, vr = _qk_prep(rows, p_att, q_norm[0], k_norm[0], ones_bd)
    att = _attention(rows, qr, kr, vr, attn_sink[0])
    r, v, kk, w0, kka0, km0, w1, kka1, km1 = _rwkv_prep(
        rows, p_rkv, p_lora, rwkv_conv[0], rwkv_w0[0], rwkv_w2[0].astype(BF16), rwkv_a0[0],
        rwkv_a2[0].astype(BF16), rwkv_k_k[0].reshape(1, W), rwkv_k_a[0].reshape(1, W), ones_bd)
    y0, y1 = _rwkv_scan(rows, r, v, kk, w0, kka0, km0, w1, kka1, km1)
    h_lat = _cd_out(rows, y0, y1, r, v, km0, km1, p_lora, att, h, mod3, 1,
                    rwkv_r_k[0].reshape(1, W), rwkv_lnx_w[0].reshape(1, W), rwkv_lnx_b[0].reshape(1, W),
                    rwkv_g_up[0].astype(BF16), cd_w_out[0].astype(BF16), ones_bd)
    lat_rows = _Rows(B, T, C)
    h_lat = _ffn(lat_rows, h_lat, norm2[1].reshape(1, D), mod3, 1, ffn_up[1].astype(BF16), ffn_conv[1],
                 ffn_down[1].astype(BF16))
    return h_lat.reshape(B, T, D)
```

```python
import functools

import numpy as np
import jax
import jax.numpy as jnp
from jax import lax
from jax.experimental import pallas as pl
from jax.experimental.pallas import tpu as pltpu

F32 = jnp.float32
BF16 = jnp.bfloat16

D = 1024
N_MOD = 6
EPS = 1e-6
GRID_W = 64
POOL_WINDOWS = (2, 4, 8, 16)
POOL_GROUP = 192
POOL_WIDTH = 768
FOURIER_WIDTH = 256
HEAD_DIM = 64
N_Q_HEADS = 8
N_KV_HEADS = 2
GQA_GROUP = 4
ATT_BLOCK = 128
ROPE_THETA = 10000.0
RWKV_HEADS = 8
RWKV_WIDTH = 512
GN_EPS = 64e-5
ATT_WIDTH = 512
KV_WIDTH = 128
D_FF = 2816
CD_SPLITS = (768, 1536, 256)

TM = 256
HALO = 8
FF_CHUNK = 256
FFT_N2 = 64
FFT_LANES = 4096
SCAN_L = 64
SCAN_HEADS = 4
VMEM_LIMIT = 56 * 1024 * 1024


def _cparams(sem):
    return pltpu.CompilerParams(dimension_semantics=sem, vmem_limit_bytes=VMEM_LIMIT)


def _resident(shape):
    nd = len(shape)
    return pl.BlockSpec(shape, lambda *_: (0,) * nd, pipeline_mode=pl.Buffered(1))


def _norm_mod(x, g, shift, scale):
    ms = jnp.mean(x * x, axis=-1, keepdims=True)
    y = x * lax.rsqrt(ms + EPS) * g
    return y * (1.0 + scale) + shift


def _seg_sum(x, ones_bd):
    hi = x.astype(BF16)
    lo = (x - hi.astype(F32)).astype(BF16)
    return (jnp.dot(hi, ones_bd, preferred_element_type=F32)
            + jnp.dot(lo, ones_bd, preferred_element_type=F32))


def _sigmoid(x):
    return 1.0 / (1.0 + jnp.exp(-x))


class _Rows:
    def __init__(self, B, T, C):
        assert T % TM == 0 and C % TM == 0 and (B * T) % C == 0
        assert T & (T - 1) == 0 and C & (C - 1) == 0
        self.B, self.T, self.C = B, T, C
        self.n_lat = B * T
        self.n_tot = B * T + B * C
        self.lat_tiles = self.n_lat // TM
        self.tot_tiles = self.n_tot // TM
        self.tiles_per_seq = T // TM

    def mod_row(self, i):
        return jnp.where(i < self.lat_tiles, i // self.tiles_per_seq, self.B)

    def seq_pos(self, i):
        is_lat = i < self.lat_tiles
        seqlen = jnp.where(is_lat, self.T, self.C)
        row0 = i * TM - jnp.where(is_lat, 0, self.n_lat)
        return lax.rem(row0, seqlen), seqlen

    def halo_specs(self, width, n_rows, col=0):
        blocks = n_rows // HALO
        per = TM // HALO
        prev = pl.BlockSpec((HALO, width), lambda i: (jnp.maximum(i * per - 1, 0), col))
        nxt = pl.BlockSpec((HALO, width), lambda i: (jnp.minimum((i + 1) * per, blocks - 1), col))
        return prev, nxt


def _dwconv3_ext(xe, w, pos0, seqlen):
    n = TM + 2 * HALO
    pos = pos0 + lax.broadcasted_iota(jnp.int32, (TM, 1), 0)
    xm = pltpu.roll(xe, 1, 0)[HALO:HALO + TM]
    xc = xe[HALO:HALO + TM]
    xn = pltpu.roll(xe, n - 1, 0)[HALO:HALO + TM]
    xm = jnp.where(pos == 0, 0.0, xm)
    xn = jnp.where(pos == seqlen - 1, 0.0, xn)
    return xm * w[0:1] + xc * w[1:2] + xn * w[2:3]


def _mod_kernel(cc_ref, w_ref, b_ref, o_ref):
    x = cc_ref[...]
    a = (x * _sigmoid(x)).astype(BF16)
    o_ref[0] = jnp.dot(a, w_ref[0].astype(BF16), preferred_element_type=F32) + b_ref[0]


def _modulation(cc, ada_w, ada_b):
    depth = ada_w.shape[0]
    nb = N_MOD * D // 1024
    return pl.pallas_call(
        _mod_kernel,
        grid=(depth, nb),
        in_specs=[pl.BlockSpec((8, D), lambda l, j: (0, 0)),
                  pl.BlockSpec((1, D, 1024), lambda l, j: (l, 0, j)),
                  pl.BlockSpec((1, 1, 1024), lambda l, j: (l, 0, j))],
        out_specs=pl.BlockSpec((1, 8, 1024), lambda l, j: (l, 0, j)),
        out_shape=jax.ShapeDtypeStruct((depth, 8, N_MOD * D), F32),
        compiler_params=_cparams(("arbitrary", "arbitrary")),
        name="adaln_mod",
    )(cc, ada_w, ada_b.reshape(depth, 1, N_MOD * D))


def _in_kernel(x_ref, g_ref, mod_ref, w_ref, *o_refs, splits):
    m = mod_ref[0]
    u = _norm_mod(x_ref[...], g_ref[...], m[:, 0:D], m[:, D:2 * D]).astype(BF16)
    off = 0
    for o_ref, n in zip(o_refs, splits):
        o_ref[...] = jnp.dot(u, w_ref[:, off:off + n], preferred_element_type=F32)
        off += n


def _in_proj(rows, h, g, mod3, layer, w_bf16, splits, name):
    n_rows = h.shape[0]
    width = w_bf16.shape[1]
    return pl.pallas_call(
        functools.partial(_in_kernel, splits=splits),
        grid=(n_rows // TM,),
        in_specs=[pl.BlockSpec((TM, D), lambda i: (i, 0)),
                  _resident((1, D)),
                  pl.BlockSpec((1, 1, N_MOD * D), lambda i: (layer * 8 + rows.mod_row(i), 0, 0)),
                  _resident((D, width))],
        out_specs=[pl.BlockSpec((TM, n), lambda i: (i, 0)) for n in splits],
        out_shape=[jax.ShapeDtypeStruct((n_rows, n), F32) for n in splits],
        compiler_params=_cparams(("arbitrary",)),
        name=name,
    )(h, g, mod3, w_bf16)


def _fft_tables(T):
    n1 = T // FFT_N2
    c = np.arange(FOURIER_WIDTH)
    ang = 2.0 * np.pi * (np.outer(c, c) % FOURIER_WIDTH) / FOURIER_WIDTH
    fc = np.concatenate([np.cos(ang), -np.sin(ang)], axis=1) / np.sqrt(FOURIER_WIDTH)
    t1 = np.arange(n1)[None, None, :]
    k1 = np.arange(n1)[None, :, None]
    t2 = np.arange(FFT_N2)[:, None, None]
    ang = 2.0 * np.pi * (((FFT_N2 * t1 + t2) * k1) % T) / T
    cm, sm = np.cos(ang) / np.sqrt(n1), np.sin(ang) / np.sqrt(n1)
    m1 = np.concatenate([np.concatenate([cm, sm], axis=2),
                         np.concatenate([-sm, cm], axis=2)], axis=1)
    k2 = np.arange(FFT_N2)
    ang = 2.0 * np.pi * (np.outer(k2, k2) % FFT_N2) / FFT_N2
    cs2 = np.concatenate([np.cos(ang), np.sin(ang)], axis=1) / np.sqrt(FFT_N2)
    return (jnp.asarray(fc, F32), jnp.asarray(m1, F32), jnp.asarray(cs2, F32))


def _ctx_fft_table(C):
    t = np.arange(C)
    ang = 2.0 * np.pi * (np.outer(t, t) % C) / C
    return jnp.asarray(np.concatenate([np.cos(ang), np.sin(ang)], axis=1) / np.sqrt(C), F32)


def _fft1_kernel(z_ref, fc_ref, m_ref, y_ref):
    a = jnp.dot(z_ref[...].astype(BF16), fc_ref[...].astype(BF16), preferred_element_type=F32)
    st = jnp.concatenate([a[:, :FOURIER_WIDTH], a[:, FOURIER_WIDTH:]], axis=0).astype(BF16)
    y_ref[0, 0] = jnp.dot(m_ref[0].astype(BF16), st, preferred_element_type=F32)


def _fft2_kernel(yr_ref, yi_ref, cs_ref, o_ref):
    st = jnp.concatenate([yr_ref[0], yi_ref[0]], axis=0).astype(BF16)
    o_ref[0] = jnp.dot(cs_ref[...].astype(BF16), st, preferred_element_type=F32)


def _ctx_fft_kernel(z_ref, fc_ref, ct_ref, o_ref):
    a = jnp.dot(z_ref[...].astype(BF16), fc_ref[...].astype(BF16), preferred_element_type=F32)
    st = jnp.concatenate([a[:, :FOURIER_WIDTH], a[:, FOURIER_WIDTH:]], axis=0).astype(BF16)
    o_ref[...] = jnp.dot(ct_ref[...].astype(BF16), st, preferred_element_type=F32)


def _fourier_spec(rows, z):
    B, T, C = rows.B, rows.T, rows.C
    n1 = T // FFT_N2
    fc, m1, cs2 = _fft_tables(T)
    zv = z.reshape(rows.n_tot // FFT_N2, FFT_N2 * D)
    lane_blocks = D // FOURIER_WIDTH
    ybuf = pl.pallas_call(
        _fft1_kernel,
        grid=(B, FFT_N2),
        in_specs=[pl.BlockSpec((n1, FOURIER_WIDTH), lambda b, t2: (b, lane_blocks * t2 + lane_blocks - 1)),
                  _resident((FOURIER_WIDTH, 2 * FOURIER_WIDTH)),
                  pl.BlockSpec((1, 2 * n1, 2 * n1), lambda b, t2: (t2, 0, 0))],
        out_specs=pl.BlockSpec((1, 1, 2 * n1, FOURIER_WIDTH), lambda b, t2: (b, t2, 0, 0)),
        out_shape=jax.ShapeDtypeStruct((B, FFT_N2, 2 * n1, FOURIER_WIDTH), F32),
        compiler_params=_cparams(("arbitrary", "arbitrary")),
        name="fft_stage1",
    )(zv, fc, m1)
    half = n1 * FOURIER_WIDTH
    lb = min(FFT_LANES, half)
    nblk = half // lb
    yv = ybuf.reshape(B, FFT_N2, 2 * half)
    spec_lat = pl.pallas_call(
        _fft2_kernel,
        grid=(B, nblk),
        in_specs=[pl.BlockSpec((1, FFT_N2, lb), lambda b, j: (b, 0, j)),
                  pl.BlockSpec((1, FFT_N2, lb), lambda b, j: (b, 0, j + nblk)),
                  _resident((FFT_N2, 2 * FFT_N2))],
        out_specs=pl.BlockSpec((1, FFT_N2, lb), lambda b, j: (b, 0, j)),
        out_shape=jax.ShapeDtypeStruct((B, FFT_N2, half), F32),
        compiler_params=_cparams(("arbitrary", "arbitrary")),
        name="fft_stage2",
    )(yv, yv, cs2)
    ct = _ctx_fft_table(C)
    spec_ctx = pl.pallas_call(
        _ctx_fft_kernel,
        grid=(B,),
        in_specs=[pl.BlockSpec((C, FOURIER_WIDTH), lambda b: (rows.n_lat // C + b, lane_blocks - 1)),
                  _resident((FOURIER_WIDTH, 2 * FOURIER_WIDTH)),
                  _resident((C, 2 * C))],
        out_specs=pl.BlockSpec((C, FOURIER_WIDTH), lambda b: (b, 0)),
        out_shape=jax.ShapeDtypeStruct((B * C, FOURIER_WIDTH), F32),
        compiler_params=_cparams(("arbitrary",)),
        name="fft_ctx",
    )(z, fc, ct)
    return jnp.concatenate([spec_lat.reshape(rows.n_lat, FOURIER_WIDTH), spec_ctx], axis=0)


def _ab_out_kernel(z_ref, zp_ref, zn_ref, spec_ref, h_ref, mod_ref, pw_ref, ps_ref, fw_ref, wo_ref,
                   o_ref, *, rows):
    i = pl.program_id(0)
    pos0, seqlen = rows.seq_pos(i)
    n = TM + 2 * HALO
    ze = jnp.concatenate([zp_ref[...], z_ref[...], zn_ref[...]], axis=0)
    pe = pos0 - HALO + lax.broadcasted_iota(jnp.int32, (n, 1), 0)
    ze = jnp.where((pe >= 0) & (pe < seqlen), ze, 0.0)
    sums = {1: ze}
    w = 1
    while w < max(POOL_WINDOWS):
        sums[2 * w] = sums[w] + pltpu.roll(sums[w], n - w, 0)
        w *= 2
    pos = pos0 + lax.broadcasted_iota(jnp.int32, (TM, 1), 0)
    lane = lax.broadcasted_iota(jnp.int32, (1, POOL_WIDTH), 1)
    zc = ze[HALO:HALO + TM]
    mean = jnp.zeros((TM, POOL_WIDTH), F32)
    for g, w in enumerate(POOL_WINDOWS):
        left = w // 2
        right = w - 1 - left
        cw = pltpu.roll(sums[w], left, 0)[HALO:HALO + TM]
        cnt = (jnp.minimum(pos + right + 1, seqlen) - jnp.maximum(pos - left, 0)).astype(F32)
        in_group = (lane >= g * POOL_GROUP) & (lane < (g + 1) * POOL_GROUP)
        mean = jnp.where(in_group, cw / cnt, mean)
    pooled = (mean - zc).astype(BF16)
    ya = jnp.dot(pooled, pw_ref[...], preferred_element_type=F32) * ps_ref[...]
    yb = jnp.dot(spec_ref[...].astype(BF16), fw_ref[...], preferred_element_type=F32)
    y = (jnp.dot(ya.astype(BF16), wo_ref[0:POOL_WIDTH, :], preferred_element_type=F32)
         + jnp.dot(yb.astype(BF16), wo_ref[POOL_WIDTH:, :], preferred_element_type=F32))
    gate = mod_ref[0][:, 2 * D:3 * D]
    o_ref[...] = h_ref[...] + gate * y


def _ab_out(rows, z, spec, h, mod3, layer, pw_bd, pscale, fw, wo):
    n_rows = rows.n_tot
    zprev, znext = rows.halo_specs(POOL_WIDTH, n_rows)
    return pl.pallas_call(
        functools.partial(_ab_out_kernel, rows=rows),
        grid=(n_rows // TM,),
        in_specs=[pl.BlockSpec((TM, POOL_WIDTH), lambda i: (i, 0)), zprev, znext,
                  pl.BlockSpec((TM, FOURIER_WIDTH), lambda i: (i, 0)),
                  pl.BlockSpec((TM, D), lambda i: (i, 0)),
                  pl.BlockSpec((1, 1, N_MOD * D), lambda i: (layer * 8 + rows.mod_row(i), 0, 0)),
                  _resident((POOL_WIDTH, POOL_WIDTH)), _resident((1, POOL_WIDTH)),
                  _resident((FOURIER_WIDTH, FOURIER_WIDTH)), _resident((D, D))],
        out_specs=pl.BlockSpec((TM, D), lambda i: (i, 0)),
        out_shape=jax.ShapeDtypeStruct((n_rows, D), F32),
        compiler_params=_cparams(("arbitrary",)),
        name="ab_out",
    )(z, z, z, spec, h, mod3, pw_bd, pscale, fw, wo)


def _ffn_kernel(h_ref, hp_ref, hn_ref, g_ref, mod_ref, wu_ref, cw_ref, wd_ref, o_ref, *, rows):
    i = pl.program_id(0)
    pos0, seqlen = rows.seq_pos(i)
    m = mod_ref[0]
    he = jnp.concatenate([hp_ref[...], h_ref[...], hn_ref[...]], axis=0)
    u = _norm_mod(he, g_ref[...], m[:, 3 * D:4 * D], m[:, 4 * D:5 * D]).astype(BF16)
    acc = jnp.zeros((TM, D), F32)
    for c in range(D_FF // FF_CHUNK):
        lo = c * FF_CHUNK
        up_g = jnp.dot(u, wu_ref[:, lo:lo + FF_CHUNK], preferred_element_type=F32)
        up_v = jnp.dot(u, wu_ref[:, D_FF + lo:D_FF + lo + FF_CHUNK], preferred_element_type=F32)
        gate = _dwconv3_ext(up_g, cw_ref[:, lo:lo + FF_CHUNK], pos0, seqlen)
        val = _dwconv3_ext(up_v, cw_ref[:, D_FF + lo:D_FF + lo + FF_CHUNK], pos0, seqlen)
        act = (gate * _sigmoid(gate) * val).astype(BF16)
        acc = acc + jnp.dot(act, wd_ref[lo:lo + FF_CHUNK, :], preferred_element_type=F32)
    o_ref[...] = h_ref[...] + m[:, 5 * D:6 * D] * acc


def _ffn(rows, h, g, mod3, layer, wu, cw, wd):
    n_rows = h.shape[0]
    hprev, hnext = rows.halo_specs(D, n_rows)
    return pl.pallas_call(
        functools.partial(_ffn_kernel, rows=rows),
        grid=(n_rows // TM,),
        in_specs=[pl.BlockSpec((TM, D), lambda i: (i, 0)), hprev, hnext,
                  _resident((1, D)),
                  pl.BlockSpec((1, 1, N_MOD * D), lambda i: (layer * 8 + rows.mod_row(i), 0, 0)),
                  _resident((D, 2 * D_FF)), _resident((3, 2 * D_FF)), _resident((D_FF, D))],
        out_specs=pl.BlockSpec((TM, D), lambda i: (i, 0)),
        out_shape=jax.ShapeDtypeStruct((n_rows, D), F32),
        compiler_params=_cparams(("arbitrary",)),
        name="conv_ffn",
    )(h, h, h, g, mod3, wu, cw, wd)


def _rope_tables(T):
    half = HEAD_DIM // 2
    inv = ROPE_THETA ** (-np.arange(0, half, 2, dtype=np.float64) / half)
    t = np.arange(T)
    ang_r = (t // GRID_W)[:, None] * inv[None, :]
    ang_c = (t % GRID_W)[:, None] * inv[None, :]
    cos = np.concatenate([np.cos(ang_r), np.cos(ang_r), np.cos(ang_c), np.cos(ang_c)], axis=1)
    sin = np.concatenate([-np.sin(ang_r), np.sin(ang_r), -np.sin(ang_c), np.sin(ang_c)], axis=1)
    cos = np.concatenate([np.tile(cos, (1, 2)), np.ones((TM, 128))], axis=0)
    sin = np.concatenate([np.tile(sin, (1, 2)), np.zeros((TM, 128))], axis=0)
    return jnp.asarray(cos, F32), jnp.asarray(sin, F32)


def _rope128(x, cos, sin):
    lane = lax.broadcasted_iota(jnp.int32, (1, 128), 1)
    partner = jnp.where((lane & 31) < 16, pltpu.roll(x, 128 - 16, 1), pltpu.roll(x, 16, 1))
    return x * cos + partner * sin


def _qk_kernel(p_ref, cos_ref, sin_ref, qn_ref, kn_ref, ones_ref, q_out, k_out, v_out):
    p = p_ref[...]
    cos, sin = cos_ref[...], sin_ref[...]
    q = p[:, 0:ATT_WIDTH]
    ms = _seg_sum(q * q, ones_ref[...]) * (1.0 / HEAD_DIM)
    q = q * lax.rsqrt(ms + EPS) * qn_ref[...]
    q = jnp.concatenate([_rope128(q[:, 128 * j:128 * (j + 1)], cos, sin) for j in range(4)], axis=1)
    q_out[...] = (q * HEAD_DIM ** -0.5).astype(BF16)
    k = p[:, ATT_WIDTH:ATT_WIDTH + KV_WIDTH]
    ms = _seg_sum(k * k, ones_ref[0:KV_WIDTH, 0:KV_WIDTH]) * (1.0 / HEAD_DIM)
    k = k * lax.rsqrt(ms + EPS) * kn_ref[...]
    k_out[...] = _rope128(k, cos, sin).astype(BF16)
    v_out[...] = p[:, ATT_WIDTH + KV_WIDTH:].astype(BF16)


def _qk_prep(rows, p_att, q_norm, k_norm, ones_bd):
    cos, sin = _rope_tables(rows.T)
    tps = rows.tiles_per_seq
    tab = lambda i: (jnp.where(i < rows.lat_tiles, lax.rem(i, tps), tps), 0)
    qn = jnp.tile(q_norm, N_Q_HEADS).reshape(1, ATT_WIDTH)
    kn = jnp.tile(k_norm, N_KV_HEADS).reshape(1, KV_WIDTH)
    return pl.pallas_call(
        _qk_kernel,
        grid=(rows.tot_tiles,),
        in_specs=[pl.BlockSpec((TM, CD_SPLITS[0]), lambda i: (i, 0)),
                  pl.BlockSpec((TM, 128), tab), pl.BlockSpec((TM, 128), tab),
                  _resident((1, ATT_WIDTH)), _resident((1, KV_WIDTH)),
                  _resident((RWKV_WIDTH, RWKV_WIDTH))],
        out_specs=[pl.BlockSpec((TM, ATT_WIDTH), lambda i: (i, 0)),
                   pl.BlockSpec((TM, KV_WIDTH), lambda i: (i, 0)),
                   pl.BlockSpec((TM, KV_WIDTH), lambda i: (i, 0))],
        out_shape=[jax.ShapeDtypeStruct((rows.n_tot, ATT_WIDTH), BF16),
                   jax.ShapeDtypeStruct((rows.n_tot, KV_WIDTH), BF16),
                   jax.ShapeDtypeStruct((rows.n_tot, KV_WIDTH), BF16)],
        compiler_params=_cparams(("arbitrary",)),
        name="qk_prep",
    )(p_att, cos, sin, qn, kn, ones_bd)


def _attn_kernel(q_ref, kp_ref, kc_ref, kn_ref, kx_ref, vp_ref, vc_ref, vn_ref, vx_ref, sink_ref,
                 o_ref, *, nq, C):
    i = pl.program_id(1)
    Q = ATT_BLOCK
    nk = 3 * Q + C
    k_all = jnp.concatenate([kp_ref[...], kc_ref[...], kn_ref[...], kx_ref[...]], axis=0)
    v_all = jnp.concatenate([vp_ref[...], vc_ref[...], vn_ref[...], vx_ref[...]], axis=0)
    row = lax.broadcasted_iota(jnp.int32, (GQA_GROUP * Q, 1), 0)
    ql = row & (Q - 1)
    grp = row >> 7
    col = lax.broadcasted_iota(jnp.int32, (1, nk), 1)
    rel = col - ql
    band = (rel >= 0) & (rel <= 2 * Q)
    kvalid = ((col >= Q) | (i > 0)) & ((col < 2 * Q) | (i < nq - 1))
    mask = (band & kvalid & (col < 3 * Q)) | (col >= 3 * Q)
    q = q_ref[...]
    outs = []
    for hk in range(N_KV_HEADS):
        q4 = jnp.concatenate(
            [q[:, HEAD_DIM * (GQA_GROUP * hk + g):HEAD_DIM * (GQA_GROUP * hk + g + 1)]
             for g in range(GQA_GROUP)], axis=0)
        kh = k_all[:, HEAD_DIM * hk:HEAD_DIM * (hk + 1)]
        vh = v_all[:, HEAD_DIM * hk:HEAD_DIM * (hk + 1)]
        s = lax.dot_general(q4, kh, (((1,), (1,)), ((), ())), preferred_element_type=F32)
        s = jnp.where(mask, s, -jnp.inf)
        sk = jnp.zeros((GQA_GROUP * Q, 1), F32)
        for g in range(GQA_GROUP):
            h = GQA_GROUP * hk + g
            sk = jnp.where(grp == g, sink_ref[h:h + 1, 0:1], sk)
        m = jnp.maximum(jnp.max(s, axis=1, keepdims=True), sk)
        p = jnp.exp(s - m)
        den = jnp.sum(p, axis=1, keepdims=True) + jnp.exp(sk - m)
        o = jnp.dot(p.astype(BF16), vh, preferred_element_type=F32) / den
        outs += [o[Q * g:Q * (g + 1)] for g in range(GQA_GROUP)]
    o_ref[...] = jnp.concatenate(outs, axis=1).astype(BF16)


def _attention(rows, qr, kr, vr, sink):
    B, T, C = rows.B, rows.T, rows.C
    nq = T // ATT_BLOCK
    blk = lambda off: pl.BlockSpec(
        (ATT_BLOCK, KV_WIDTH), lambda b, i: (b * nq + jnp.clip(i + off, 0, nq - 1), 0))
    ctx = pl.BlockSpec((C, KV_WIDTH), lambda b, i: (rows.n_lat // C + b, 0))
    sink_b = jnp.broadcast_to(sink.reshape(N_Q_HEADS, 1), (N_Q_HEADS, 128))
    return pl.pallas_call(
        functools.partial(_attn_kernel, nq=nq, C=C),
        grid=(B, nq),
        in_specs=[pl.BlockSpec((ATT_BLOCK, ATT_WIDTH), lambda b, i: (b * nq + i, 0)),
                  blk(-1), blk(0), blk(1), ctx, blk(-1), blk(0), blk(1), ctx,
                  _resident((N_Q_HEADS, 128))],
        out_specs=pl.BlockSpec((ATT_BLOCK, ATT_WIDTH), lambda b, i: (b * nq + i, 0)),
        out_shape=jax.ShapeDtypeStruct((rows.n_lat, ATT_WIDTH), BF16),
        compiler_params=_cparams(("arbitrary", "arbitrary")),
        name="window_attention",
    )(qr, kr, kr, kr, kr, vr, vr, vr, vr, sink_b)


def _rwkv_prep_kernel(x_ref, xp_ref, xn_ref, lora_ref, cw_ref, w0_ref, w2_ref, a0_ref, a2_ref,
                      kk_ref, ka_ref, ones_ref,
                      r_out, v_out, kkn_out, w_out0, kka_out0, km_out0, w_out1, kka_out1, km_out1, *, rows):
    i = pl.program_id(0)
    pos0, seqlen = rows.seq_pos(i)
    xe = jnp.concatenate([xp_ref[...], x_ref[...], xn_ref[...]], axis=0)
    rkv = _dwconv3_ext(xe, cw_ref[...], pos0, seqlen)
    W = RWKV_WIDTH
    r, k, v = rkv[:, 0:W], rkv[:, W:2 * W], rkv[:, 2 * W:3 * W]
    kk = k * kk_ref[...]
    kk = kk * lax.rsqrt(_seg_sum(kk * kk, ones_ref[...]) + EPS)
    r_out[...] = r
    v_out[...] = v
    kkn_out[...] = kk
    lora = lora_ref[...]
    wl = jnp.tanh(lora[:, 0:64]).astype(BF16)
    al = lora[:, 64:128].astype(BF16)
    outs = ((w_out0, kka_out0, km_out0), (w_out1, kka_out1, km_out1))
    for d in range(2):
        x = -(w0_ref[d:d + 1, :] + jnp.dot(wl, w2_ref[d], preferred_element_type=F32))
        softplus = jnp.maximum(x, 0.0) + jnp.log(1.0 + jnp.exp(-jnp.abs(x)))
        w_log = -softplus - 0.5
        a = _sigmoid(a0_ref[d:d + 1, :] + jnp.dot(al, a2_ref[d], preferred_element_type=F32))
        w_o, kka_o, km_o = outs[d]
        w_o[...] = -jnp.exp(w_log)
        kka_o[...] = kk * a
        km_o[...] = k * (1.0 + (a - 1.0) * ka_ref[...])


def _rwkv_prep(rows, p_rkv, p_lora, conv_w, w0, w2, a0, a2, k_k, k_a, ones_bd):
    n_rows = rows.n_tot
    W = RWKV_WIDTH
    xprev, xnext = rows.halo_specs(3 * W, n_rows)
    out_spec = pl.BlockSpec((TM, W), lambda i: (i, 0))
    return pl.pallas_call(
        functools.partial(_rwkv_prep_kernel, rows=rows),
        grid=(rows.tot_tiles,),
        in_specs=[pl.BlockSpec((TM, 3 * W), lambda i: (i, 0)), xprev, xnext,
                  pl.BlockSpec((TM, CD_SPLITS[2]), lambda i: (i, 0)),
                  _resident((3, 3 * W)), _resident((2, W)), _resident((2, 64, W)),
                  _resident((2, W)), _resident((2, 64, W)), _resident((1, W)), _resident((1, W)),
                  _resident((W, W))],
        out_specs=[out_spec] * 9,
        out_shape=[jax.ShapeDtypeStruct((n_rows, W), F32)] * 9,
        compiler_params=_cparams(("arbitrary",)),
        name="rwkv_prep",
    )(p_rkv, p_rkv, p_rkv, p_lora, conv_w, w0, w2, a0, a2, k_k, k_a, ones_bd)


def _bd(x, bd_mask):
    xb = x.astype(BF16)
    return jnp.where(bd_mask, jnp.concatenate([xb] * SCAN_HEADS, axis=0), jnp.zeros((), BF16))


def _mm(a, b):
    return jnp.dot(a.astype(BF16), b, preferred_element_type=F32)


def _diag_blocks(f):
    lane_head = lax.broadcasted_iota(jnp.int32, (SCAN_L, 256), 1) >> 6
    out = jnp.zeros((SCAN_L, 256), F32)
    for h in range(SCAN_HEADS):
        out = jnp.where(lane_head == h, f[SCAN_L * h:SCAN_L * (h + 1)], out)
    return out


def _chunk_step(chains, states, reverse):
    L = SCAN_L
    n = len(chains)
    row = lax.broadcasted_iota(jnp.int32, (L, 256), 0)
    idx = lax.broadcasted_iota(jnp.int32, (L, 256), 1) & (L - 1)
    incl = (idx >= row) if reverse else (idx <= row)
    strict = (idx > row) if reverse else (idx < row)
    eye = (idx == row).astype(F32)
    bd_mask = ((lax.broadcasted_iota(jnp.int32, (256, 256), 0) >> 6)
               == (lax.broadcasted_iota(jnp.int32, (256, 256), 1) >> 6))
    r64 = lax.broadcasted_iota(jnp.int32, (L, L), 0)
    c64 = lax.broadcasted_iota(jnp.int32, (L, L), 1)
    tri = ((c64 >= r64) if reverse else (c64 <= r64)).astype(BF16)
    nt = (((1,), (1,)), ((), ()))
    tn = (((0,), (0,)), ((), ()))
    bd = lambda x: _bd(x, bd_mask)
    last = 0 if reverse else L - 1

    def cum_log_decay(lw):
        h1 = lw.astype(BF16)
        r1 = lw - h1.astype(F32)
        h2 = r1.astype(BF16)
        h3 = (r1 - h2.astype(F32)).astype(BF16)
        return (jnp.dot(tri, h1, preferred_element_type=F32) + jnp.dot(tri, h2, preferred_element_type=F32)
                + jnp.dot(tri, h3, preferred_element_type=F32))

    lcs = [cum_log_decay(c[3]) for c in chains]
    pre = []
    for (r, v, kk, lw, kka, km), lc in zip(chains, lcs):
        e_in = jnp.exp(lc)
        e_neg = jnp.exp(-lc)
        p_end = e_in[last:last + 1]
        bb = kka * e_neg
        kb = km * e_neg
        pre.append(dict(ab=-kk * jnp.exp(lc - lw), bb=bb, kb=kb, rb=r * e_in, bh=bb * p_end, kh=kb * p_end,
                        p_end=p_end, v=v))
    lhs = [jnp.concatenate([p["ab"], p["rb"]], axis=0).astype(BF16) for p in pre]
    g1 = [lax.dot_general(l, bd(p["bb"]), nt, preferred_element_type=F32) for l, p in zip(lhs, pre)]
    g2 = [lax.dot_general(l, bd(p["kb"]), nt, preferred_element_type=F32) for l, p in zip(lhs, pre)]
    aab = [jnp.where(strict, g[:L], 0.0) for g in g1]
    brb = [jnp.where(incl, g[L:], 0.0) for g in g1]
    aak = [jnp.where(strict, g[:L], 0.0) for g in g2]
    brk = [jnp.where(incl, g[L:], 0.0) for g in g2]

    def coupling(m):
        sh = m.bit_length() - 1
        tb, sb = row >> sh, idx >> sh
        later, earlier = (sb, tb) if reverse else (tb, sb)
        return ((row >> (sh + 1)) == (idx >> (sh + 1))) & ((later & 1) == 1) & ((earlier & 1) == 0)

    cm = coupling(1)
    tw = [eye + jnp.where(cm, a, 0.0) for a in aab]
    m = 2
    while m < L:
        cm = coupling(m)
        x = [_mm(t, bd(jnp.where(cm, a, 0.0))) for t, a in zip(tw, aab)]
        tw = [t + _mm(xi, bd(t)) for t, xi in zip(tw, x)]
        m *= 2

    v_bd = [bd(p["v"]) for p in pre]
    av = [_mm(a, vb) for a, vb in zip(aak, v_bd)]
    wu = [_mm(t, jnp.concatenate([bd(p["ab"]), bd(a)], axis=1)) for t, p, a in zip(tw, pre, av)]
    qy = [_mm(b, jnp.concatenate([bd(w[:, :256]), bd(w[:, 256:])], axis=1)) for b, w in zip(brb, wu)]
    y2b = [_mm(b, vb) for b, vb in zip(brk, v_bd)]
    f1 = [lax.dot_general(p["bh"].astype(BF16), w.astype(BF16), tn, preferred_element_type=F32)
          for p, w in zip(pre, wu)]
    f2 = [lax.dot_general(p["kh"].astype(BF16), p["v"].astype(BF16), tn, preferred_element_type=F32)
          for p in pre]
    ys, new_states = [], []
    for i in range(n):
        q = pre[i]["rb"] + qy[i][:, :256]
        y2 = qy[i][:, 256:] + y2b[i]
        mw = eye * pre[i]["p_end"] + _diag_blocks(f1[i][:, :256])
        nw = _diag_blocks(f1[i][:, 256:]) + _diag_blocks(f2[i])
        sw = states[i]
        s_hi = sw.astype(BF16)
        s_lo = sw - s_hi.astype(F32)
        qm = jnp.concatenate([q, mw], axis=0).astype(BF16)
        z = (jnp.dot(qm, bd(s_hi), preferred_element_type=F32)
             + jnp.dot(qm, bd(s_lo), preferred_element_type=F32))
        ys.append(z[:L] + y2)
        new_states.append(z[L:] + nw)
    return ys, new_states


def _chunk_scan_kernel(*refs, reverse, n_batch):
    ins = [refs[6 * b:6 * (b + 1)] for b in range(n_batch)]
    y_ref = refs[6 * n_batch]
    s_ref = refs[6 * n_batch + 1]

    @pl.when(pl.program_id(0) == 0)
    def _():
        s_ref[...] = jnp.zeros_like(s_ref)

    groups = [(b, slice(256 * g, 256 * (g + 1))) for b in range(n_batch)
              for g in range(RWKV_HEADS // SCAN_HEADS)]
    chains = [tuple(ref[:, sl] for ref in ins[b]) for b, sl in groups]
    states = [s_ref[b, :, sl] for b, sl in groups]
    ys, new_states = _chunk_step(chains, states, reverse)
    for (b, sl), y, s_new in zip(groups, ys, new_states):
        y_ref[b, :, sl] = y
        s_ref[b, :, sl] = s_new


def _rwkv_scan_dir(rows, r, v, kk, lw, kka, km, reverse):
    B, T, C = rows.B, rows.T, rows.C
    L = SCAN_L
    nctx, nlat = C // L, T // L

    def in_blk(c, b):
        j_ctx = (nctx - 1 - c) if reverse else c
        j_lat = (nlat - 1 - (c - nctx)) if reverse else (c - nctx)
        return (jnp.where(c < nctx, rows.n_lat // L + b * nctx + j_ctx, b * nlat + j_lat), 0)

    def out_blk(c):
        j = jnp.maximum(c - nctx, 0)
        return (0, (nlat - 1 - j) if reverse else j, 0)

    in_specs, args = [], []
    for b in range(B):
        in_specs += [pl.BlockSpec((L, RWKV_WIDTH), functools.partial(in_blk, b=b))] * 6
        args += [r, v, kk, lw, kka, km]
    y = pl.pallas_call(
        functools.partial(_chunk_scan_kernel, reverse=reverse, n_batch=B),
        grid=(nctx + nlat,),
        in_specs=in_specs,
        out_specs=pl.BlockSpec((B, L, RWKV_WIDTH), out_blk),
        out_shape=jax.ShapeDtypeStruct((B, T, RWKV_WIDTH), F32),
        scratch_shapes=[pltpu.VMEM((B, HEAD_DIM, RWKV_WIDTH), F32)],
        compiler_params=_cparams(("arbitrary",)),
        name="rwkv_scan_rev" if reverse else "rwkv_scan_fwd",
    )(*args)
    return y.reshape(rows.n_lat, RWKV_WIDTH)


def _cd_out_kernel(y0_ref, y1_ref, r_ref, v_ref, km0_ref, km1_ref, lora_ref, att_ref, h_ref, mod_ref,
                   rk_ref, lw_ref, lb_ref, gu_ref, wo_ref, ones_ref, o_ref):
    ones_bd = ones_ref[...]
    inv = 1.0 / HEAD_DIM
    y = y0_ref[...] + y1_ref[...]
    mu = _seg_sum(y, ones_bd) * inv
    yc = y - mu
    var = _seg_sum(yc * yc, ones_bd) * inv
    yn = yc * lax.rsqrt(var + GN_EPS) * lw_ref[...] + lb_ref[...]
    r = r_ref[...]
    rk = rk_ref[...]
    bonus = (_seg_sum(r * km0_ref[...] * rk, ones_bd) + _seg_sum(r * km1_ref[...] * rk, ones_bd)) * v_ref[...]
    gate = jnp.dot(_sigmoid(lora_ref[...][:, 128:256]).astype(BF16), gu_ref[...], preferred_element_type=F32)
    mix = ((yn + bonus) * gate).astype(BF16)
    out = (jnp.dot(att_ref[...], wo_ref[0:ATT_WIDTH, :], preferred_element_type=F32)
           + jnp.dot(mix, wo_ref[ATT_WIDTH:, :], preferred_element_type=F32))
    o_ref[...] = h_ref[...] + mod_ref[0][:, 2 * D:3 * D] * out


def _cd_out(rows, y0, y1, r, v, km0, km1, p_lora, att, h, mod3, layer, r_k, lnx_w, lnx_b, g_up, wo, ones_bd):
    W = RWKV_WIDTH
    tile = lambda n: pl.BlockSpec((TM, n), lambda i: (i, 0))
    return pl.pallas_call(
        _cd_out_kernel,
        grid=(rows.lat_tiles,),
        in_specs=[tile(W)] * 6 + [tile(CD_SPLITS[2]), tile(ATT_WIDTH), tile(D),
                  pl.BlockSpec((1, 1, N_MOD * D), lambda i: (layer * 8 + rows.mod_row(i), 0, 0)),
                  _resident((1, W)), _resident((1, W)), _resident((1, W)),
                  _resident((128, W)), _resident((D, D)), _resident((W, W))],
        out_specs=tile(D),
        out_shape=jax.ShapeDtypeStruct((rows.n_lat, D), F32),
        compiler_params=_cparams(("arbitrary",)),
        name="cd_out",
    )(y0, y1, r, v, km0, km1, p_lora, att, h, mod3, r_k, lnx_w, lnx_b, g_up, wo, ones_bd)


def _block_diag(blocks):
    n = blocks.shape[0]
    g = blocks.shape[1]
    out = jnp.zeros((n * g, n * g), blocks.dtype)
    for j in range(n):
        out = out.at[j * g:(j + 1) * g, j * g:(j + 1) * g].set(blocks[j])
    return out


def kernel(x, c, ctx, c_ctx, ada_w, ada_b, norm1, norm2, ffn_up, ffn_conv, ffn_down, ab_w_in, pool_w,
           pool_scale, fourier_w, ab_w_out, cd_w_in, q_norm, k_norm, attn_sink, rwkv_conv, rwkv_w0,
           rwkv_w2, rwkv_a0, rwkv_a2, rwkv_k_k, rwkv_k_a, rwkv_r_k, rwkv_lnx_w, rwkv_lnx_b,
           rwkv_g_up, cd_w_out):
    B, T, _ = x.shape
    C = ctx.shape[1]
    depth = ada_w.shape[0]
    assert depth == 2 and B + 1 <= 8
    rows = _Rows(B, T, C)
    W = RWKV_WIDTH

    cc = jnp.zeros((8, D), F32).at[:B].set(c).at[B].set(c_ctx)
    mod3 = _modulation(cc, ada_w, ada_b).reshape(depth * 8, 1, N_MOD * D)
    ones_bd = _block_diag(jnp.ones((RWKV_HEADS, HEAD_DIM, HEAD_DIM), BF16))

    h = jnp.concatenate([x.reshape(B * T, D), ctx.reshape(B * C, D)], axis=0)

    (z,) = _in_proj(rows, h, norm1[0].reshape(1, D), mod3, 0, ab_w_in[0].astype(BF16), (D,), "ab_in")
    spec = _fourier_spec(rows, z)
    h = _ab_out(rows, z, spec, h, mod3, 0, _block_diag(pool_w[0]).astype(BF16),
                pool_scale[0].reshape(1, POOL_WIDTH), fourier_w[0].astype(BF16), ab_w_out[0].astype(BF16))
    h = _ffn(rows, h, norm2[0].reshape(1, D), mod3, 0, ffn_up[0].astype(BF16), ffn_conv[0],
             ffn_down[0].astype(BF16))

    p_att, p_rkv, p_lora = _in_proj(rows, h, norm1[1].reshape(1, D), mod3, 1, cd_w_in[0].astype(BF16),
                                    CD_SPLITS, "cd_in")
    qr, kr, vr = _qk_prep(rows, p_att, q_norm[0], k_norm[0], ones_bd)
    att = _attention(rows, qr, kr, vr, attn_sink[0])
    r, v, kk, w0, kka0, km0, w1, kka1, km1 = _rwkv_prep(
        rows, p_rkv, p_lora, rwkv_conv[0], rwkv_w0[0], rwkv_w2[0].astype(BF16), rwkv_a0[0],
        rwkv_a2[0].astype(BF16), rwkv_k_k[0].reshape(1, W), rwkv_k_a[0].reshape(1, W), ones_bd)
    y0 = _rwkv_scan_dir(rows, r, v, kk, w0, kka0, km0, False)
    y1 = _rwkv_scan_dir(rows, r, v, kk, w1, kka1, km1, True)
    h_lat = _cd_out(rows, y0, y1, r, v, km0, km1, p_lora, att, h, mod3, 1,
                    rwkv_r_k[0].reshape(1, W), rwkv_lnx_w[0].reshape(1, W), rwkv_lnx_b[0].reshape(1, W),
                    rwkv_g_up[0].astype(BF16), cd_w_out[0].astype(BF16), ones_bd)
    lat_rows = _Rows(B, T, C)
    h_lat = _ffn(lat_rows, h_lat, norm2[1].reshape(1, D), mod3, 1, ffn_up[1].astype(BF16), ffn_conv[1],
                 ffn_down[1].astype(BF16))
    return h_lat.reshape(B, T, D)
```

```python
import functools

import numpy as np
import jax
import jax.numpy as jnp
from jax import lax
from jax.experimental import pallas as pl
from jax.experimental.pallas import tpu as pltpu

F32 = jnp.float32
BF16 = jnp.bfloat16

D = 1024
N_MOD = 6
EPS = 1e-6
GRID_W = 64
POOL_WINDOWS = (2, 4, 8, 16)
POOL_GROUP = 192
POOL_WIDTH = 768
FOURIER_WIDTH = 256
HEAD_DIM = 64
N_Q_HEADS = 8
N_KV_HEADS = 2
GQA_GROUP = 4
ATT_BLOCK = 128
ROPE_THETA = 10000.0
RWKV_HEADS = 8
RWKV_WIDTH = 512
GN_EPS = 64e-5
ATT_WIDTH = 512
KV_WIDTH = 128
D_FF = 2816
CD_SPLITS = (768, 1536, 256)

TM = 256
HALO = 8
FF_CHUNK = 256
FFT_N2 = 64
FFT_SUB = 8
SCAN_L = 64
SCAN_HEADS = 4
VMEM_LIMIT = 56 * 1024 * 1024


def _cparams(sem):
    return pltpu.CompilerParams(dimension_semantics=sem, vmem_limit_bytes=VMEM_LIMIT)


def _resident(shape):
    nd = len(shape)
    return pl.BlockSpec(shape, lambda *_: (0,) * nd, pipeline_mode=pl.Buffered(1))


def _norm_mod(x, g, shift, scale):
    ms = jnp.mean(x * x, axis=-1, keepdims=True)
    y = x * lax.rsqrt(ms + EPS) * g
    return y * (1.0 + scale) + shift


def _seg_sum(x, ones_bd):
    hi = x.astype(BF16)
    lo = (x - hi.astype(F32)).astype(BF16)
    return (jnp.dot(hi, ones_bd, preferred_element_type=F32)
            + jnp.dot(lo, ones_bd, preferred_element_type=F32))


def _sigmoid(x):
    return 1.0 / (1.0 + jnp.exp(-x))


class _Rows:
    def __init__(self, B, T, C):
        assert T % TM == 0 and C % TM == 0 and (B * T) % C == 0
        assert T & (T - 1) == 0 and C & (C - 1) == 0
        self.B, self.T, self.C = B, T, C
        self.n_lat = B * T
        self.n_tot = B * T + B * C
        self.lat_tiles = self.n_lat // TM
        self.tot_tiles = self.n_tot // TM
        self.tiles_per_seq = T // TM

    def mod_row(self, i):
        return jnp.where(i < self.lat_tiles, i // self.tiles_per_seq, self.B)

    def seq_pos(self, i):
        is_lat = i < self.lat_tiles
        seqlen = jnp.where(is_lat, self.T, self.C)
        row0 = i * TM - jnp.where(is_lat, 0, self.n_lat)
        return lax.rem(row0, seqlen), seqlen

    def halo_specs(self, width, n_rows, col=0):
        blocks = n_rows // HALO
        per = TM // HALO
        prev = pl.BlockSpec((HALO, width), lambda i: (jnp.maximum(i * per - 1, 0), col))
        nxt = pl.BlockSpec((HALO, width), lambda i: (jnp.minimum((i + 1) * per, blocks - 1), col))
        return prev, nxt


def _dwconv3_ext(xe, w, pos0, seqlen):
    n = TM + 2 * HALO
    pos = pos0 + lax.broadcasted_iota(jnp.int32, (TM, 1), 0)
    xm = pltpu.roll(xe, 1, 0)[HALO:HALO + TM]
    xc = xe[HALO:HALO + TM]
    xn = pltpu.roll(xe, n - 1, 0)[HALO:HALO + TM]
    xm = jnp.where(pos == 0, 0.0, xm)
    xn = jnp.where(pos == seqlen - 1, 0.0, xn)
    return xm * w[0:1] + xc * w[1:2] + xn * w[2:3]


def _mod_kernel(cc_ref, w_ref, b_ref, o_ref):
    x = cc_ref[...]
    a = (x * _sigmoid(x)).astype(BF16)
    o_ref[0] = jnp.dot(a, w_ref[0].astype(BF16), preferred_element_type=F32) + b_ref[0]


def _modulation(cc, ada_w, ada_b):
    depth = ada_w.shape[0]
    nb = N_MOD * D // 1024
    return pl.pallas_call(
        _mod_kernel,
        grid=(depth, nb),
        in_specs=[pl.BlockSpec((8, D), lambda l, j: (0, 0)),
                  pl.BlockSpec((1, D, 1024), lambda l, j: (l, 0, j)),
                  pl.BlockSpec((1, 1, 1024), lambda l, j: (l, 0, j))],
        out_specs=pl.BlockSpec((1, 8, 1024), lambda l, j: (l, 0, j)),
        out_shape=jax.ShapeDtypeStruct((depth, 8, N_MOD * D), F32),
        compiler_params=_cparams(("arbitrary", "arbitrary")),
        name="adaln_mod",
    )(cc, ada_w, ada_b.reshape(depth, 1, N_MOD * D))


def _in_kernel(x_ref, g_ref, mod_ref, w_ref, *o_refs, splits):
    m = mod_ref[0]
    u = _norm_mod(x_ref[...], g_ref[...], m[:, 0:D], m[:, D:2 * D]).astype(BF16)
    off = 0
    for o_ref, n in zip(o_refs, splits):
        o_ref[...] = jnp.dot(u, w_ref[:, off:off + n], preferred_element_type=F32)
        off += n


def _in_proj(rows, h, g, mod3, layer, w_bf16, splits, name):
    n_rows = h.shape[0]
    width = w_bf16.shape[1]
    return pl.pallas_call(
        functools.partial(_in_kernel, splits=splits),
        grid=(n_rows // TM,),
        in_specs=[pl.BlockSpec((TM, D), lambda i: (i, 0)),
                  _resident((1, D)),
                  pl.BlockSpec((1, 1, N_MOD * D), lambda i: (layer * 8 + rows.mod_row(i), 0, 0)),
                  _resident((D, width))],
        out_specs=[pl.BlockSpec((TM, n), lambda i: (i, 0)) for n in splits],
        out_shape=[jax.ShapeDtypeStruct((n_rows, n), F32) for n in splits],
        compiler_params=_cparams(("arbitrary",)),
        name=name,
    )(h, g, mod3, w_bf16)


def _fft_tables(T):
    n1 = T // FFT_N2
    c = np.arange(FOURIER_WIDTH)
    ang = 2.0 * np.pi * (np.outer(c, c) % FOURIER_WIDTH) / FOURIER_WIDTH
    fc = np.concatenate([np.cos(ang), -np.sin(ang)], axis=1) / np.sqrt(FOURIER_WIDTH)
    t1 = np.arange(n1)[None, None, :]
    k1 = np.arange(n1)[None, :, None]
    t2 = np.arange(FFT_N2)[:, None, None]
    ang = 2.0 * np.pi * (((FFT_N2 * t1 + t2) * k1) % T) / T
    cm, sm = np.cos(ang) / np.sqrt(n1), np.sin(ang) / np.sqrt(n1)
    m1 = np.concatenate([np.concatenate([cm, sm], axis=2),
                         np.concatenate([-sm, cm], axis=2)], axis=1)
    k2 = np.arange(FFT_N2)
    ang = 2.0 * np.pi * (np.outer(k2, k2) % FFT_N2) / FFT_N2
    cs2 = np.concatenate([np.cos(ang), np.sin(ang)], axis=1) / np.sqrt(FFT_N2)
    return (jnp.asarray(fc, F32), jnp.asarray(m1, F32), jnp.asarray(cs2, F32))


def _ctx_fft_table(C):
    t = np.arange(C)
    ang = 2.0 * np.pi * (np.outer(t, t) % C) / C
    return jnp.asarray(np.concatenate([np.cos(ang), np.sin(ang)], axis=1) / np.sqrt(C), F32)


def _fft1_kernel(z_ref, fc_ref, m_ref, y_ref):
    fc = fc_ref[...].astype(BF16)
    for j in range(FFT_SUB):
        a = jnp.dot(z_ref[:, j, :].astype(BF16), fc, preferred_element_type=F32)
        st = jnp.concatenate([a[:, :FOURIER_WIDTH], a[:, FOURIER_WIDTH:]], axis=0).astype(BF16)
        y_ref[0, j] = jnp.dot(m_ref[j].astype(BF16), st, preferred_element_type=F32)


def _fft2_kernel(yr_ref, yi_ref, cs_ref, o_ref):
    cs = cs_ref[...].astype(BF16)
    for j in range(FFT_SUB):
        st = jnp.concatenate([yr_ref[0, :, j, :], yi_ref[0, :, j, :]], axis=0).astype(BF16)
        o_ref[0, :, j, :] = jnp.dot(cs, st, preferred_element_type=F32)


def _ctx_fft_kernel(z_ref, fc_ref, ct_ref, o_ref):
    a = jnp.dot(z_ref[...].astype(BF16), fc_ref[...].astype(BF16), preferred_element_type=F32)
    st = jnp.concatenate([a[:, :FOURIER_WIDTH], a[:, FOURIER_WIDTH:]], axis=0).astype(BF16)
    o_ref[...] = jnp.dot(ct_ref[...].astype(BF16), st, preferred_element_type=F32)


def _fourier_spec(rows, z):
    B, T, C = rows.B, rows.T, rows.C
    n1 = T // FFT_N2
    fc, m1, cs2 = _fft_tables(T)
    zv = z.reshape(rows.n_tot // FFT_N2, FFT_N2, D)
    lane_blocks = D // FOURIER_WIDTH
    ybuf = pl.pallas_call(
        _fft1_kernel,
        grid=(B, FFT_N2 // FFT_SUB),
        in_specs=[pl.BlockSpec((n1, FFT_SUB, FOURIER_WIDTH), lambda b, j: (b, j, lane_blocks - 1)),
                  _resident((FOURIER_WIDTH, 2 * FOURIER_WIDTH)),
                  pl.BlockSpec((FFT_SUB, 2 * n1, 2 * n1), lambda b, j: (j, 0, 0))],
        out_specs=pl.BlockSpec((1, FFT_SUB, 2 * n1, FOURIER_WIDTH), lambda b, j: (b, j, 0, 0)),
        out_shape=jax.ShapeDtypeStruct((B, FFT_N2, 2 * n1, FOURIER_WIDTH), F32),
        compiler_params=_cparams(("arbitrary", "arbitrary")),
        name="fft_stage1",
    )(zv, fc, m1)
    nblk = n1 // FFT_SUB
    spec_lat = pl.pallas_call(
        _fft2_kernel,
        grid=(B, nblk),
        in_specs=[pl.BlockSpec((1, FFT_N2, FFT_SUB, FOURIER_WIDTH), lambda b, j: (b, 0, j, 0)),
                  pl.BlockSpec((1, FFT_N2, FFT_SUB, FOURIER_WIDTH), lambda b, j: (b, 0, j + nblk, 0)),
                  _resident((FFT_N2, 2 * FFT_N2))],
        out_specs=pl.BlockSpec((1, FFT_N2, FFT_SUB, FOURIER_WIDTH), lambda b, j: (b, 0, j, 0)),
        out_shape=jax.ShapeDtypeStruct((B, FFT_N2, n1, FOURIER_WIDTH), F32),
        compiler_params=_cparams(("arbitrary", "arbitrary")),
        name="fft_stage2",
    )(ybuf, ybuf, cs2)
    ct = _ctx_fft_table(C)
    spec_ctx = pl.pallas_call(
        _ctx_fft_kernel,
        grid=(B,),
        in_specs=[pl.BlockSpec((C, FOURIER_WIDTH), lambda b: (rows.n_lat // C + b, lane_blocks - 1)),
                  _resident((FOURIER_WIDTH, 2 * FOURIER_WIDTH)),
                  _resident((C, 2 * C))],
        out_specs=pl.BlockSpec((C, FOURIER_WIDTH), lambda b: (b, 0)),
        out_shape=jax.ShapeDtypeStruct((B * C, FOURIER_WIDTH), F32),
        compiler_params=_cparams(("arbitrary",)),
        name="fft_ctx",
    )(z, fc, ct)
    return jnp.concatenate([spec_lat.reshape(rows.n_lat, FOURIER_WIDTH), spec_ctx], axis=0)


def _ab_out_kernel(z_ref, zp_ref, zn_ref, spec_ref, h_ref, mod_ref, pw_ref, ps_ref, fw_ref, wo_ref,
                   o_ref, *, rows):
    i = pl.program_id(0)
    pos0, seqlen = rows.seq_pos(i)
    n = TM + 2 * HALO
    ze = jnp.concatenate([zp_ref[...], z_ref[...], zn_ref[...]], axis=0)
    pe = pos0 - HALO + lax.broadcasted_iota(jnp.int32, (n, 1), 0)
    ze = jnp.where((pe >= 0) & (pe < seqlen), ze, 0.0)
    sums = {1: ze}
    w = 1
    while w < max(POOL_WINDOWS):
        sums[2 * w] = sums[w] + pltpu.roll(sums[w], n - w, 0)
        w *= 2
    pos = pos0 + lax.broadcasted_iota(jnp.int32, (TM, 1), 0)
    lane = lax.broadcasted_iota(jnp.int32, (1, POOL_WIDTH), 1)
    zc = ze[HALO:HALO + TM]
    mean = jnp.zeros((TM, POOL_WIDTH), F32)
    for g, w in enumerate(POOL_WINDOWS):
        left = w // 2
        right = w - 1 - left
        cw = pltpu.roll(sums[w], left, 0)[HALO:HALO + TM]
        cnt = (jnp.minimum(pos + right + 1, seqlen) - jnp.maximum(pos - left, 0)).astype(F32)
        in_group = (lane >= g * POOL_GROUP) & (lane < (g + 1) * POOL_GROUP)
        mean = jnp.where(in_group, cw / cnt, mean)
    pooled = (mean - zc).astype(BF16)
    ya = jnp.dot(pooled, pw_ref[...], preferred_element_type=F32) * ps_ref[...]
    yb = jnp.dot(spec_ref[...].astype(BF16), fw_ref[...], preferred_element_type=F32)
    y = (jnp.dot(ya.astype(BF16), wo_ref[0:POOL_WIDTH, :], preferred_element_type=F32)
         + jnp.dot(yb.astype(BF16), wo_ref[POOL_WIDTH:, :], preferred_element_type=F32))
    gate = mod_ref[0][:, 2 * D:3 * D]
    o_ref[...] = h_ref[...] + gate * y


def _ab_out(rows, z, spec, h, mod3, layer, pw_bd, pscale, fw, wo):
    n_rows = rows.n_tot
    zprev, znext = rows.halo_specs(POOL_WIDTH, n_rows)
    return pl.pallas_call(
        functools.partial(_ab_out_kernel, rows=rows),
        grid=(n_rows // TM,),
        in_specs=[pl.BlockSpec((TM, POOL_WIDTH), lambda i: (i, 0)), zprev, znext,
                  pl.BlockSpec((TM, FOURIER_WIDTH), lambda i: (i, 0)),
                  pl.BlockSpec((TM, D), lambda i: (i, 0)),
                  pl.BlockSpec((1, 1, N_MOD * D), lambda i: (layer * 8 + rows.mod_row(i), 0, 0)),
                  _resident((POOL_WIDTH, POOL_WIDTH)), _resident((1, POOL_WIDTH)),
                  _resident((FOURIER_WIDTH, FOURIER_WIDTH)), _resident((D, D))],
        out_specs=pl.BlockSpec((TM, D), lambda i: (i, 0)),
        out_shape=jax.ShapeDtypeStruct((n_rows, D), F32),
        compiler_params=_cparams(("arbitrary",)),
        name="ab_out",
    )(z, z, z, spec, h, mod3, pw_bd, pscale, fw, wo)


def _ffn_kernel(h_ref, hp_ref, hn_ref, g_ref, mod_ref, wu_ref, cw_ref, wd_ref, o_ref, up_ref, act_ref, *, rows):
    i = pl.program_id(0)
    pos0, seqlen = rows.seq_pos(i)
    n = TM + 2 * HALO
    m = mod_ref[0]
    he = jnp.concatenate([hp_ref[...], h_ref[...], hn_ref[...]], axis=0)
    u = _norm_mod(he, g_ref[...], m[:, 3 * D:4 * D], m[:, 4 * D:5 * D])
    pe = pos0 - HALO + lax.broadcasted_iota(jnp.int32, (n, 1), 0)
    u = jnp.where((pe >= 0) & (pe < seqlen), u, 0.0).astype(BF16)

    def conv(slot, col):
        w = cw_ref[:, col:col + FF_CHUNK]
        return (up_ref[slot, HALO - 1:HALO - 1 + TM, :] * w[0:1] + up_ref[slot, HALO:HALO + TM, :] * w[1:2]
                + up_ref[slot, HALO + 1:HALO + 1 + TM, :] * w[2:3])

    for c in range(D_FF // FF_CHUNK):
        lo = c * FF_CHUNK
        sg, sv = 2 * (c % 2), 2 * (c % 2) + 1
        up_ref[sg] = jnp.dot(u, wu_ref[:, lo:lo + FF_CHUNK], preferred_element_type=F32)
        up_ref[sv] = jnp.dot(u, wu_ref[:, D_FF + lo:D_FF + lo + FF_CHUNK], preferred_element_type=F32)
        gate = conv(sg, lo)
        val = conv(sv, D_FF + lo)
        half = 0.5 * gate
        act_ref[:, lo:lo + FF_CHUNK] = ((half + half * jnp.tanh(half)) * val).astype(BF16)
    acc = jnp.dot(act_ref[...], wd_ref[...], preferred_element_type=F32)
    o_ref[...] = h_ref[...] + m[:, 5 * D:6 * D] * acc


def _ffn(rows, h, g, mod3, layer, wu, cw, wd):
    n_rows = h.shape[0]
    hprev, hnext = rows.halo_specs(D, n_rows)
    return pl.pallas_call(
        functools.partial(_ffn_kernel, rows=rows),
        grid=(n_rows // TM,),
        in_specs=[pl.BlockSpec((TM, D), lambda i: (i, 0)), hprev, hnext,
                  _resident((1, D)),
                  pl.BlockSpec((1, 1, N_MOD * D), lambda i: (layer * 8 + rows.mod_row(i), 0, 0)),
                  _resident((D, 2 * D_FF)), _resident((3, 2 * D_FF)), _resident((D_FF, D))],
        out_specs=pl.BlockSpec((TM, D), lambda i: (i, 0)),
        out_shape=jax.ShapeDtypeStruct((n_rows, D), F32),
        scratch_shapes=[pltpu.VMEM((4, TM + 2 * HALO, FF_CHUNK), F32), pltpu.VMEM((TM, D_FF), BF16)],
        compiler_params=_cparams(("arbitrary",)),
        name="conv_ffn",
    )(h, h, h, g, mod3, wu, cw, wd)


def _rope_tables(T):
    half = HEAD_DIM // 2
    inv = ROPE_THETA ** (-np.arange(0, half, 2, dtype=np.float64) / half)
    t = np.arange(T)
    ang_r = (t // GRID_W)[:, None] * inv[None, :]
    ang_c = (t % GRID_W)[:, None] * inv[None, :]
    cos = np.concatenate([np.cos(ang_r), np.cos(ang_r), np.cos(ang_c), np.cos(ang_c)], axis=1)
    sin = np.concatenate([-np.sin(ang_r), np.sin(ang_r), -np.sin(ang_c), np.sin(ang_c)], axis=1)
    cos = np.concatenate([np.tile(cos, (1, 2)), np.ones((TM, 128))], axis=0)
    sin = np.concatenate([np.tile(sin, (1, 2)), np.zeros((TM, 128))], axis=0)
    return jnp.asarray(cos, F32), jnp.asarray(sin, F32)


def _rope128(x, cos, sin):
    lane = lax.broadcasted_iota(jnp.int32, (1, 128), 1)
    partner = jnp.where((lane & 31) < 16, pltpu.roll(x, 128 - 16, 1), pltpu.roll(x, 16, 1))
    return x * cos + partner * sin


def _cd_in_kernel(x_ref, g_ref, mod_ref, w_ref, cos_ref, sin_ref, qn_ref, kn_ref, ones_ref,
                  q_out, k_out, v_out, rkv_out, lora_out):
    m = mod_ref[0]
    u = _norm_mod(x_ref[...], g_ref[...], m[:, 0:D], m[:, D:2 * D]).astype(BF16)
    n_att, n_rkv, n_lora = CD_SPLITS
    rkv_out[...] = jnp.dot(u, w_ref[:, n_att:n_att + n_rkv], preferred_element_type=F32)
    lora_out[...] = jnp.dot(u, w_ref[:, n_att + n_rkv:], preferred_element_type=F32)
    p = jnp.dot(u, w_ref[:, 0:n_att], preferred_element_type=F32)
    cos, sin = cos_ref[...], sin_ref[...]
    q = p[:, 0:ATT_WIDTH]
    ms = _seg_sum(q * q, ones_ref[...]) * (1.0 / HEAD_DIM)
    q = q * lax.rsqrt(ms + EPS) * qn_ref[...]
    q = jnp.concatenate([_rope128(q[:, 128 * j:128 * (j + 1)], cos, sin) for j in range(4)], axis=1)
    q_out[...] = (q * HEAD_DIM ** -0.5).astype(BF16)
    k = p[:, ATT_WIDTH:ATT_WIDTH + KV_WIDTH]
    ms = _seg_sum(k * k, ones_ref[0:KV_WIDTH, 0:KV_WIDTH]) * (1.0 / HEAD_DIM)
    k = k * lax.rsqrt(ms + EPS) * kn_ref[...]
    k_out[...] = _rope128(k, cos, sin).astype(BF16)
    v_out[...] = p[:, ATT_WIDTH + KV_WIDTH:].astype(BF16)


def _cd_in(rows, h, g, mod3, layer, w_bf16, q_norm, k_norm, ones_bd):
    cos, sin = _rope_tables(rows.T)
    tps = rows.tiles_per_seq
    tab = lambda i: (jnp.where(i < rows.lat_tiles, lax.rem(i, tps), tps), 0)
    qn = jnp.tile(q_norm, N_Q_HEADS).reshape(1, ATT_WIDTH)
    kn = jnp.tile(k_norm, N_KV_HEADS).reshape(1, KV_WIDTH)
    widths = (ATT_WIDTH, KV_WIDTH, KV_WIDTH, CD_SPLITS[1], CD_SPLITS[2])
    dtypes = (BF16, BF16, BF16, F32, F32)
    return pl.pallas_call(
        _cd_in_kernel,
        grid=(rows.tot_tiles,),
        in_specs=[pl.BlockSpec((TM, D), lambda i: (i, 0)),
                  _resident((1, D)),
                  pl.BlockSpec((1, 1, N_MOD * D), lambda i: (layer * 8 + rows.mod_row(i), 0, 0)),
                  _resident((D, sum(CD_SPLITS))),
                  pl.BlockSpec((TM, 128), tab), pl.BlockSpec((TM, 128), tab),
                  _resident((1, ATT_WIDTH)), _resident((1, KV_WIDTH)),
                  _resident((RWKV_WIDTH, RWKV_WIDTH))],
        out_specs=[pl.BlockSpec((TM, n), lambda i: (i, 0)) for n in widths],
        out_shape=[jax.ShapeDtypeStruct((rows.n_tot, n), dt) for n, dt in zip(widths, dtypes)],
        compiler_params=_cparams(("arbitrary",)),
        name="cd_in",
    )(h, g, mod3, w_bf16, cos, sin, qn, kn, ones_bd)


def _attn_bias(C):
    Q = ATT_BLOCK
    ql = (np.arange(GQA_GROUP * Q) % Q)[:, None]
    col = np.arange(3 * Q + C)[None, :]
    rel = col - ql
    band = (rel >= 0) & (rel <= 2 * Q) & (col < 3 * Q)
    is_ctx = np.broadcast_to(col >= 3 * Q, band.shape)
    variants = [band & (col >= Q) | is_ctx, band | is_ctx, band & (col < 2 * Q) | is_ctx]
    return jnp.asarray(np.where(np.stack(variants), 0.0, -1e30), F32)


def _attn_kernel(q_ref, kp_ref, kc_ref, kn_ref, kx_ref, vp_ref, vc_ref, vn_ref, vx_ref, sink_ref, bias_ref,
                 o_ref, *, nq):
    i = pl.program_id(1)
    Q = ATT_BLOCK
    heads = range(N_KV_HEADS)
    k_all = jnp.concatenate([kp_ref[...], kc_ref[...], kn_ref[...], kx_ref[...]], axis=0)
    v_all = jnp.concatenate([vp_ref[...], vc_ref[...], vn_ref[...], vx_ref[...]], axis=0)
    bias = bias_ref[jnp.where(i == 0, 0, jnp.where(i == nq - 1, 2, 1))]
    q = q_ref[...]
    q4 = [jnp.concatenate([q[:, HEAD_DIM * (GQA_GROUP * hk + g):HEAD_DIM * (GQA_GROUP * hk + g + 1)]
                           for g in range(GQA_GROUP)], axis=0) for hk in heads]
    kh = [k_all[:, HEAD_DIM * hk:HEAD_DIM * (hk + 1)] for hk in heads]
    vh = [v_all[:, HEAD_DIM * hk:HEAD_DIM * (hk + 1)] for hk in heads]
    s = [lax.dot_general(q4[hk], kh[hk], (((1,), (1,)), ((), ())), preferred_element_type=F32) + bias
         for hk in heads]
    sk = [sink_ref[hk][:, 0:1] for hk in heads]
    m = [jnp.maximum(jnp.max(s[hk], axis=1, keepdims=True), sk[hk]) for hk in heads]
    p = [jnp.exp(s[hk] - m[hk]) for hk in heads]
    den = [jnp.sum(p[hk], axis=1, keepdims=True) + jnp.exp(sk[hk] - m[hk]) for hk in heads]
    o = [jnp.dot(p[hk].astype(BF16), vh[hk], preferred_element_type=F32) * (1.0 / den[hk]) for hk in heads]
    o_ref[...] = jnp.concatenate([o[hk][Q * g:Q * (g + 1)] for hk in heads for g in range(GQA_GROUP)],
                                 axis=1).astype(BF16)


def _attention(rows, qr, kr, vr, sink):
    B, T, C = rows.B, rows.T, rows.C
    nq = T // ATT_BLOCK
    assert nq >= 2
    blk = lambda off: pl.BlockSpec(
        (ATT_BLOCK, KV_WIDTH), lambda b, i: (b * nq + jnp.clip(i + off, 0, nq - 1), 0))
    ctx = pl.BlockSpec((C, KV_WIDTH), lambda b, i: (rows.n_lat // C + b, 0))
    sink_col = jnp.repeat(sink.reshape(N_KV_HEADS, GQA_GROUP), ATT_BLOCK, axis=1)
    sink_col = jnp.broadcast_to(sink_col[:, :, None], (N_KV_HEADS, GQA_GROUP * ATT_BLOCK, 128))
    return pl.pallas_call(
        functools.partial(_attn_kernel, nq=nq),
        grid=(B, nq),
        in_specs=[pl.BlockSpec((ATT_BLOCK, ATT_WIDTH), lambda b, i: (b * nq + i, 0)),
                  blk(-1), blk(0), blk(1), ctx, blk(-1), blk(0), blk(1), ctx,
                  _resident((N_KV_HEADS, GQA_GROUP * ATT_BLOCK, 128)),
                  _resident((3, GQA_GROUP * ATT_BLOCK, 3 * ATT_BLOCK + C))],
        out_specs=pl.BlockSpec((ATT_BLOCK, ATT_WIDTH), lambda b, i: (b * nq + i, 0)),
        out_shape=jax.ShapeDtypeStruct((rows.n_lat, ATT_WIDTH), BF16),
        compiler_params=_cparams(("arbitrary", "arbitrary")),
        name="window_attention",
    )(qr, kr, kr, kr, kr, vr, vr, vr, vr, sink_col, _attn_bias(C))


def _rwkv_prep_kernel(x_ref, xp_ref, xn_ref, lora_ref, cw_ref, w0_ref, w2_ref, a0_ref, a2_ref,
                      kk_ref, ka_ref, rk_ref, ones_ref,
                      r_out, v_out, kkn_out, w_out0, kka_out0, km_out0, w_out1, kka_out1, km_out1, bonus_out,
                      *, rows):
    i = pl.program_id(0)
    pos0, seqlen = rows.seq_pos(i)
    xe = jnp.concatenate([xp_ref[...], x_ref[...], xn_ref[...]], axis=0)
    rkv = _dwconv3_ext(xe, cw_ref[...], pos0, seqlen)
    W = RWKV_WIDTH
    r, k, v = rkv[:, 0:W], rkv[:, W:2 * W], rkv[:, 2 * W:3 * W]
    kk = k * kk_ref[...]
    kk = kk * lax.rsqrt(_seg_sum(kk * kk, ones_ref[...]) + EPS)
    r_out[...] = r
    v_out[...] = v
    kkn_out[...] = kk
    lora = lora_ref[...]
    wl = jnp.tanh(lora[:, 0:64]).astype(BF16)
    al = lora[:, 64:128].astype(BF16)
    outs = ((w_out0, kka_out0, km_out0), (w_out1, kka_out1, km_out1))
    bonus = jnp.zeros_like(r)
    for d in range(2):
        x = -(w0_ref[d:d + 1, :] + jnp.dot(wl, w2_ref[d], preferred_element_type=F32))
        softplus = jnp.maximum(x, 0.0) + jnp.log(1.0 + jnp.exp(-jnp.abs(x)))
        w_log = -softplus - 0.5
        a = _sigmoid(a0_ref[d:d + 1, :] + jnp.dot(al, a2_ref[d], preferred_element_type=F32))
        w_o, kka_o, km_o = outs[d]
        w_o[...] = -jnp.exp(w_log)
        kka_o[...] = kk * a
        km = k * (1.0 + (a - 1.0) * ka_ref[...])
        km_o[...] = km
        bonus = bonus + _seg_sum(r * km * rk_ref[...], ones_ref[...])
    bonus_out[...] = bonus * v


def _rwkv_prep(rows, p_rkv, p_lora, conv_w, w0, w2, a0, a2, k_k, k_a, r_k, ones_bd):
    n_rows = rows.n_tot
    W = RWKV_WIDTH
    xprev, xnext = rows.halo_specs(3 * W, n_rows)
    out_spec = pl.BlockSpec((TM, W), lambda i: (i, 0))
    return pl.pallas_call(
        functools.partial(_rwkv_prep_kernel, rows=rows),
        grid=(rows.tot_tiles,),
        in_specs=[pl.BlockSpec((TM, 3 * W), lambda i: (i, 0)), xprev, xnext,
                  pl.BlockSpec((TM, CD_SPLITS[2]), lambda i: (i, 0)),
                  _resident((3, 3 * W)), _resident((2, W)), _resident((2, 64, W)),
                  _resident((2, W)), _resident((2, 64, W)), _resident((1, W)), _resident((1, W)),
                  _resident((1, W)), _resident((W, W))],
        out_specs=[out_spec] * 10,
        out_shape=[jax.ShapeDtypeStruct((n_rows, W), F32)] * 10,
        compiler_params=_cparams(("arbitrary",)),
        name="rwkv_prep",
    )(p_rkv, p_rkv, p_rkv, p_lora, conv_w, w0, w2, a0, a2, k_k, k_a, r_k, ones_bd)


def _bd(x, bd_mask):
    xb = x.astype(BF16)
    return jnp.where(bd_mask, jnp.concatenate([xb] * SCAN_HEADS, axis=0), jnp.zeros((), BF16))


def _mm(a, b):
    return jnp.dot(a.astype(BF16), b, preferred_element_type=F32)


def _diag_blocks(f):
    lane_head = lax.broadcasted_iota(jnp.int32, (SCAN_L, 256), 1) >> 6
    out = jnp.zeros((SCAN_L, 256), F32)
    for h in range(SCAN_HEADS):
        out = jnp.where(lane_head == h, f[SCAN_L * h:SCAN_L * (h + 1)], out)
    return out


def _chunk_step(chains, states, reverse):
    L = SCAN_L
    n = len(chains)
    row = lax.broadcasted_iota(jnp.int32, (L, 256), 0)
    idx = lax.broadcasted_iota(jnp.int32, (L, 256), 1) & (L - 1)
    incl = (idx >= row) if reverse else (idx <= row)
    strict = (idx > row) if reverse else (idx < row)
    eye = (idx == row).astype(F32)
    bd_mask = ((lax.broadcasted_iota(jnp.int32, (256, 256), 0) >> 6)
               == (lax.broadcasted_iota(jnp.int32, (256, 256), 1) >> 6))
    r64 = lax.broadcasted_iota(jnp.int32, (L, L), 0)
    c64 = lax.broadcasted_iota(jnp.int32, (L, L), 1)
    tri = ((c64 >= r64) if reverse else (c64 <= r64)).astype(BF16)
    nt = (((1,), (1,)), ((), ()))
    tn = (((0,), (0,)), ((), ()))
    bd = lambda x: _bd(x, bd_mask)
    last = 0 if reverse else L - 1

    def cum_log_decay(lw):
        h1 = lw.astype(BF16)
        r1 = lw - h1.astype(F32)
        h2 = r1.astype(BF16)
        h3 = (r1 - h2.astype(F32)).astype(BF16)
        return (jnp.dot(tri, h1, preferred_element_type=F32) + jnp.dot(tri, h2, preferred_element_type=F32)
                + jnp.dot(tri, h3, preferred_element_type=F32))

    lcs = [cum_log_decay(c[3]) for c in chains]
    pre = []
    for (r, v, kk, lw, kka, km), lc in zip(chains, lcs):
        e_in = jnp.exp(lc)
        e_neg = jnp.exp(-lc)
        p_end = e_in[last:last + 1]
        bb = kka * e_neg
        kb = km * e_neg
        pre.append(dict(ab=-kk * jnp.exp(lc - lw), bb=bb, kb=kb, rb=r * e_in, bh=bb * p_end, kh=kb * p_end,
                        p_end=p_end, v=v))
    lhs = [jnp.concatenate([p["ab"], p["rb"]], axis=0).astype(BF16) for p in pre]
    g1 = [lax.dot_general(l, bd(p["bb"]), nt, preferred_element_type=F32) for l, p in zip(lhs, pre)]
    g2 = [lax.dot_general(l, bd(p["kb"]), nt, preferred_element_type=F32) for l, p in zip(lhs, pre)]
    aab = [jnp.where(strict, g[:L], 0.0) for g in g1]
    brb = [jnp.where(incl, g[L:], 0.0) for g in g1]
    aak = [jnp.where(strict, g[:L], 0.0) for g in g2]
    brk = [jnp.where(incl, g[L:], 0.0) for g in g2]

    def coupling(m):
        sh = m.bit_length() - 1
        tb, sb = row >> sh, idx >> sh
        later, earlier = (sb, tb) if reverse else (tb, sb)
        return ((row >> (sh + 1)) == (idx >> (sh + 1))) & ((later & 1) == 1) & ((earlier & 1) == 0)

    cm = coupling(1)
    tw = [eye + jnp.where(cm, a, 0.0) for a in aab]
    m = 2
    while m < L:
        cm = coupling(m)
        x = [_mm(t, bd(jnp.where(cm, a, 0.0))) for t, a in zip(tw, aab)]
        tw = [t + _mm(xi, bd(t)) for t, xi in zip(tw, x)]
        m *= 2

    v_bd = [bd(p["v"]) for p in pre]
    av = [_mm(a, vb) for a, vb in zip(aak, v_bd)]
    wu = [_mm(t, jnp.concatenate([bd(p["ab"]), bd(a)], axis=1)) for t, p, a in zip(tw, pre, av)]
    qy = [_mm(b, jnp.concatenate([bd(w[:, :256]), bd(w[:, 256:])], axis=1)) for b, w in zip(brb, wu)]
    y2b = [_mm(b, vb) for b, vb in zip(brk, v_bd)]
    f1 = [lax.dot_general(p["bh"].astype(BF16), w.astype(BF16), tn, preferred_element_type=F32)
          for p, w in zip(pre, wu)]
    f2 = [lax.dot_general(p["kh"].astype(BF16), p["v"].astype(BF16), tn, preferred_element_type=F32)
          for p in pre]
    ys, new_states = [], []
    for i in range(n):
        q = pre[i]["rb"] + qy[i][:, :256]
        y2 = qy[i][:, 256:] + y2b[i]
        mw = eye * pre[i]["p_end"] + _diag_blocks(f1[i][:, :256])
        nw = _diag_blocks(f1[i][:, 256:]) + _diag_blocks(f2[i])
        sw = states[i]
        s_hi = sw.astype(BF16)
        s_lo = sw - s_hi.astype(F32)
        qm = jnp.concatenate([q, mw], axis=0).astype(BF16)
        z = (jnp.dot(qm, bd(s_hi), preferred_element_type=F32)
             + jnp.dot(qm, bd(s_lo), preferred_element_type=F32))
        ys.append(z[:L] + y2)
        new_states.append(z[L:] + nw)
    return ys, new_states


def _chunk_scan_kernel(*refs, reverse, n_batch):
    ins = [refs[6 * b:6 * (b + 1)] for b in range(n_batch)]
    y_ref = refs[6 * n_batch]
    s_ref = refs[6 * n_batch + 1]

    @pl.when(pl.program_id(0) == 0)
    def _():
        s_ref[...] = jnp.zeros_like(s_ref)

    groups = [(b, slice(256 * g, 256 * (g + 1))) for b in range(n_batch)
              for g in range(RWKV_HEADS // SCAN_HEADS)]
    chains = [tuple(ref[:, sl] for ref in ins[b]) for b, sl in groups]
    states = [s_ref[b, :, sl] for b, sl in groups]
    ys, new_states = _chunk_step(chains, states, reverse)
    for (b, sl), y, s_new in zip(groups, ys, new_states):
        y_ref[b, :, sl] = y
        s_ref[b, :, sl] = s_new


def _rwkv_scan_dir(rows, r, v, kk, lw, kka, km, reverse):
    B, T, C = rows.B, rows.T, rows.C
    L = SCAN_L
    nctx, nlat = C // L, T // L

    def in_blk(c, b):
        j_ctx = (nctx - 1 - c) if reverse else c
        j_lat = (nlat - 1 - (c - nctx)) if reverse else (c - nctx)
        return (jnp.where(c < nctx, rows.n_lat // L + b * nctx + j_ctx, b * nlat + j_lat), 0)

    def out_blk(c):
        j = jnp.maximum(c - nctx, 0)
        return (0, (nlat - 1 - j) if reverse else j, 0)

    in_specs, args = [], []
    for b in range(B):
        in_specs += [pl.BlockSpec((L, RWKV_WIDTH), functools.partial(in_blk, b=b))] * 6
        args += [r, v, kk, lw, kka, km]
    y = pl.pallas_call(
        functools.partial(_chunk_scan_kernel, reverse=reverse, n_batch=B),
        grid=(nctx + nlat,),
        in_specs=in_specs,
        out_specs=pl.BlockSpec((B, L, RWKV_WIDTH), out_blk),
        out_shape=jax.ShapeDtypeStruct((B, T, RWKV_WIDTH), F32),
        scratch_shapes=[pltpu.VMEM((B, HEAD_DIM, RWKV_WIDTH), F32)],
        compiler_params=_cparams(("arbitrary",)),
        name="rwkv_scan_rev" if reverse else "rwkv_scan_fwd",
    )(*args)
    return y.reshape(rows.n_lat, RWKV_WIDTH)


def _cd_out_kernel(y0_ref, y1_ref, bonus_ref, lora_ref, att_ref, h_ref, mod_ref,
                   lw_ref, lb_ref, gu_ref, wo_ref, ones_ref, o_ref):
    ones_bd = ones_ref[...]
    inv = 1.0 / HEAD_DIM
    y = y0_ref[...] + y1_ref[...]
    mu = _seg_sum(y, ones_bd) * inv
    yc = y - mu
    var = _seg_sum(yc * yc, ones_bd) * inv
    yn = yc * lax.rsqrt(var + GN_EPS) * lw_ref[...] + lb_ref[...]
    gate = jnp.dot(_sigmoid(lora_ref[...][:, 128:256]).astype(BF16), gu_ref[...], preferred_element_type=F32)
    mix = ((yn + bonus_ref[...]) * gate).astype(BF16)
    out = (jnp.dot(att_ref[...], wo_ref[0:ATT_WIDTH, :], preferred_element_type=F32)
           + jnp.dot(mix, wo_ref[ATT_WIDTH:, :], preferred_element_type=F32))
    o_ref[...] = h_ref[...] + mod_ref[0][:, 2 * D:3 * D] * out


def _cd_out(rows, y0, y1, bonus, p_lora, att, h, mod3, layer, lnx_w, lnx_b, g_up, wo, ones_bd):
    W = RWKV_WIDTH
    tile = lambda n: pl.BlockSpec((TM, n), lambda i: (i, 0))
    return pl.pallas_call(
        _cd_out_kernel,
        grid=(rows.lat_tiles,),
        in_specs=[tile(W)] * 3 + [tile(CD_SPLITS[2]), tile(ATT_WIDTH), tile(D),
                  pl.BlockSpec((1, 1, N_MOD * D), lambda i: (layer * 8 + rows.mod_row(i), 0, 0)),
                  _resident((1, W)), _resident((1, W)),
                  _resident((128, W)), _resident((D, D)), _resident((W, W))],
        out_specs=tile(D),
        out_shape=jax.ShapeDtypeStruct((rows.n_lat, D), F32),
        compiler_params=_cparams(("arbitrary",)),
        name="cd_out",
    )(y0, y1, bonus, p_lora, att, h, mod3, lnx_w, lnx_b, g_up, wo, ones_bd)


def _block_diag(blocks):
    n = blocks.shape[0]
    g = blocks.shape[1]
    out = jnp.zeros((n * g, n * g), blocks.dtype)
    for j in range(n):
        out = out.at[j * g:(j + 1) * g, j * g:(j + 1) * g].set(blocks[j])
    return out


def kernel(x, c, ctx, c_ctx, ada_w, ada_b, norm1, norm2, ffn_up, ffn_conv, ffn_down, ab_w_in, pool_w,
           pool_scale, fourier_w, ab_w_out, cd_w_in, q_norm, k_norm, attn_sink, rwkv_conv, rwkv_w0,
           rwkv_w2, rwkv_a0, rwkv_a2, rwkv_k_k, rwkv_k_a, rwkv_r_k, rwkv_lnx_w, rwkv_lnx_b,
           rwkv_g_up, cd_w_out):
    B, T, _ = x.shape
    C = ctx.shape[1]
    depth = ada_w.shape[0]
    assert depth == 2 and B + 1 <= 8
    rows = _Rows(B, T, C)
    W = RWKV_WIDTH

    cc = jnp.zeros((8, D), F32).at[:B].set(c).at[B].set(c_ctx)
    mod3 = _modulation(cc, ada_w, ada_b).reshape(depth * 8, 1, N_MOD * D)
    ones_bd = _block_diag(jnp.ones((RWKV_HEADS, HEAD_DIM, HEAD_DIM), BF16))

    h = jnp.concatenate([x.reshape(B * T, D), ctx.reshape(B * C, D)], axis=0)

    (z,) = _in_proj(rows, h, norm1[0].reshape(1, D), mod3, 0, ab_w_in[0].astype(BF16), (D,), "ab_in")
    spec = _fourier_spec(rows, z)
    h = _ab_out(rows, z, spec, h, mod3, 0, _block_diag(pool_w[0]).astype(BF16),
                pool_scale[0].reshape(1, POOL_WIDTH), fourier_w[0].astype(BF16), ab_w_out[0].astype(BF16))
    h = _ffn(rows, h, norm2[0].reshape(1, D), mod3, 0, ffn_up[0].astype(BF16), ffn_conv[0],
             ffn_down[0].astype(BF16))

    qr, kr, vr, p_rkv, p_lora = _cd_in(rows, h, norm1[1].reshape(1, D), mod3, 1, cd_w_in[0].astype(BF16),
                                       q_norm[0], k_norm[0], ones_bd)
    att = _attention(rows, qr, kr, vr, attn_sink[0])
    r, v, kk, w0, kka0, km0, w1, kka1, km1, bonus = _rwkv_prep(
        rows, p_rkv, p_lora, rwkv_conv[0], rwkv_w0[0], rwkv_w2[0].astype(BF16), rwkv_a0[0],
        rwkv_a2[0].astype(BF16), rwkv_k_k[0].reshape(1, W), rwkv_k_a[0].reshape(1, W),
        rwkv_r_k[0].reshape(1, W), ones_bd)
    y0 = _rwkv_scan_dir(rows, r, v, kk, w0, kka0, km0, False)
    y1 = _rwkv_scan_dir(rows, r, v, kk, w1, kka1, km1, True)
    h_lat = _cd_out(rows, y0, y1, bonus, p_lora, att, h, mod3, 1,
                    rwkv_lnx_w[0].reshape(1, W), rwkv_lnx_b[0].reshape(1, W),
                    rwkv_g_up[0].astype(BF16), cd_w_out[0].astype(BF16), ones_bd)
    lat_rows = _Rows(B, T, C)
    h_lat = _ffn(lat_rows, h_lat, norm2[1].reshape(1, D), mod3, 1, ffn_up[1].astype(BF16), ffn_conv[1],
                 ffn_down[1].astype(BF16))
    return h_lat.reshape(B, T, D)
```

```python
import functools

import numpy as np
import jax
import jax.numpy as jnp
from jax import lax
from jax.experimental import pallas as pl
from jax.experimental.pallas import tpu as pltpu

F32 = jnp.float32
BF16 = jnp.bfloat16

D = 1024
N_MOD = 6
EPS = 1e-6
GRID_W = 64
POOL_WINDOWS = (2, 4, 8, 16)
POOL_GROUP = 192
POOL_WIDTH = 768
FOURIER_WIDTH = 256
HEAD_DIM = 64
N_Q_HEADS = 8
N_KV_HEADS = 2
GQA_GROUP = 4
ATT_BLOCK = 128
ROPE_THETA = 10000.0
RWKV_HEADS = 8
RWKV_WIDTH = 512
GN_EPS = 64e-5
ATT_WIDTH = 512
KV_WIDTH = 128
D_FF = 2816
CD_SPLITS = (768, 1536, 256)

TM = 256
HALO = 8
FF_CHUNK = 256
FFT_N2 = 64
FFT_SUB = 8
SCAN_L = 64
SCAN_HEADS = 4
VMEM_LIMIT = 56 * 1024 * 1024


def _cparams(sem):
    return pltpu.CompilerParams(dimension_semantics=sem, vmem_limit_bytes=VMEM_LIMIT)


def _resident(shape):
    nd = len(shape)
    return pl.BlockSpec(shape, lambda *_: (0,) * nd, pipeline_mode=pl.Buffered(1))


def _norm_mod(x, g, shift, scale):
    ms = jnp.mean(x * x, axis=-1, keepdims=True)
    y = x * lax.rsqrt(ms + EPS) * g
    return y * (1.0 + scale) + shift


def _seg_sum(x, ones_bd):
    hi = x.astype(BF16)
    lo = (x - hi.astype(F32)).astype(BF16)
    return (jnp.dot(hi, ones_bd, preferred_element_type=F32)
            + jnp.dot(lo, ones_bd, preferred_element_type=F32))


def _sigmoid(x):
    return 1.0 / (1.0 + jnp.exp(-x))


class _Rows:
    def __init__(self, B, T, C):
        assert T % TM == 0 and C % TM == 0 and (B * T) % C == 0
        assert T & (T - 1) == 0 and C & (C - 1) == 0
        self.B, self.T, self.C = B, T, C
        self.n_lat = B * T
        self.n_tot = B * T + B * C
        self.lat_tiles = self.n_lat // TM
        self.tot_tiles = self.n_tot // TM
        self.tiles_per_seq = T // TM

    def mod_row(self, i):
        return jnp.where(i < self.lat_tiles, i // self.tiles_per_seq, self.B)

    def seq_pos(self, i):
        is_lat = i < self.lat_tiles
        seqlen = jnp.where(is_lat, self.T, self.C)
        row0 = i * TM - jnp.where(is_lat, 0, self.n_lat)
        return lax.rem(row0, seqlen), seqlen

    def halo_specs(self, width, n_rows, col=0):
        blocks = n_rows // HALO
        per = TM // HALO
        prev = pl.BlockSpec((HALO, width), lambda i: (jnp.maximum(i * per - 1, 0), col))
        nxt = pl.BlockSpec((HALO, width), lambda i: (jnp.minimum((i + 1) * per, blocks - 1), col))
        return prev, nxt


def _dwconv3_ext(xe, w, pos0, seqlen):
    n = TM + 2 * HALO
    pos = pos0 + lax.broadcasted_iota(jnp.int32, (TM, 1), 0)
    xm = pltpu.roll(xe, 1, 0)[HALO:HALO + TM]
    xc = xe[HALO:HALO + TM]
    xn = pltpu.roll(xe, n - 1, 0)[HALO:HALO + TM]
    xm = jnp.where(pos == 0, 0.0, xm)
    xn = jnp.where(pos == seqlen - 1, 0.0, xn)
    return xm * w[0:1] + xc * w[1:2] + xn * w[2:3]


def _mod_kernel(cc_ref, w_ref, b_ref, o_ref):
    x = cc_ref[...]
    a = (x * _sigmoid(x)).astype(BF16)
    o_ref[0] = jnp.dot(a, w_ref[0].astype(BF16), preferred_element_type=F32) + b_ref[0]


def _modulation(cc, ada_w, ada_b):
    depth = ada_w.shape[0]
    nb = N_MOD * D // 1024
    return pl.pallas_call(
        _mod_kernel,
        grid=(depth, nb),
        in_specs=[pl.BlockSpec((8, D), lambda l, j: (0, 0)),
                  pl.BlockSpec((1, D, 1024), lambda l, j: (l, 0, j)),
                  pl.BlockSpec((1, 1, 1024), lambda l, j: (l, 0, j))],
        out_specs=pl.BlockSpec((1, 8, 1024), lambda l, j: (l, 0, j)),
        out_shape=jax.ShapeDtypeStruct((depth, 8, N_MOD * D), F32),
        compiler_params=_cparams(("arbitrary", "arbitrary")),
        name="adaln_mod",
    )(cc, ada_w, ada_b.reshape(depth, 1, N_MOD * D))


def _split_rows_specs(rows, width):
    lat = pl.BlockSpec((TM, width), lambda i: (jnp.minimum(i, rows.lat_tiles - 1), 0))
    ctx = pl.BlockSpec((TM, width), lambda i: (jnp.maximum(i - rows.lat_tiles, 0), 0))
    return lat, ctx


def _pick_rows(rows, lat_ref, ctx_ref):
    return jnp.where(pl.program_id(0) < rows.lat_tiles, lat_ref[...], ctx_ref[...])


def _in_kernel(x_ref, c_ref, g_ref, mod_ref, w_ref, o_ref, *, rows):
    m = mod_ref[0]
    u = _norm_mod(_pick_rows(rows, x_ref, c_ref), g_ref[...], m[:, 0:D], m[:, D:2 * D]).astype(BF16)
    o_ref[...] = jnp.dot(u, w_ref[...], preferred_element_type=F32)


def _in_proj(rows, x_lat, x_ctx, g, mod3, layer, w_bf16, name):
    width = w_bf16.shape[1]
    lat_spec, ctx_spec = _split_rows_specs(rows, D)
    return pl.pallas_call(
        functools.partial(_in_kernel, rows=rows),
        grid=(rows.tot_tiles,),
        in_specs=[lat_spec, ctx_spec,
                  _resident((1, D)),
                  pl.BlockSpec((1, 1, N_MOD * D), lambda i: (layer * 8 + rows.mod_row(i), 0, 0)),
                  _resident((D, width))],
        out_specs=pl.BlockSpec((TM, width), lambda i: (i, 0)),
        out_shape=jax.ShapeDtypeStruct((rows.n_tot, width), F32),
        compiler_params=_cparams(("arbitrary",)),
        name=name,
    )(x_lat, x_ctx, g, mod3, w_bf16)


def _fft_tables(T):
    n1 = T // FFT_N2
    c = np.arange(FOURIER_WIDTH)
    ang = 2.0 * np.pi * (np.outer(c, c) % FOURIER_WIDTH) / FOURIER_WIDTH
    fc = np.concatenate([np.cos(ang), -np.sin(ang)], axis=1) / np.sqrt(FOURIER_WIDTH)
    t1 = np.arange(n1)[None, None, :]
    k1 = np.arange(n1)[None, :, None]
    t2 = np.arange(FFT_N2)[:, None, None]
    ang = 2.0 * np.pi * (((FFT_N2 * t1 + t2) * k1) % T) / T
    cm, sm = np.cos(ang) / np.sqrt(n1), np.sin(ang) / np.sqrt(n1)
    m1 = np.concatenate([np.concatenate([cm, sm], axis=2),
                         np.concatenate([-sm, cm], axis=2)], axis=1)
    k2 = np.arange(FFT_N2)
    ang = 2.0 * np.pi * (np.outer(k2, k2) % FFT_N2) / FFT_N2
    cs2 = np.concatenate([np.cos(ang), np.sin(ang)], axis=1) / np.sqrt(FFT_N2)
    return (jnp.asarray(fc, F32), jnp.asarray(m1, F32), jnp.asarray(cs2, F32))


def _ctx_fft_table(C):
    t = np.arange(C)
    ang = 2.0 * np.pi * (np.outer(t, t) % C) / C
    return jnp.asarray(np.concatenate([np.cos(ang), np.sin(ang)], axis=1) / np.sqrt(C), F32)


def _fft1_kernel(z_ref, fc_ref, m_ref, y_ref):
    fc = fc_ref[...].astype(BF16)
    for j in range(FFT_SUB):
        a = jnp.dot(z_ref[:, j, :].astype(BF16), fc, preferred_element_type=F32)
        st = jnp.concatenate([a[:, :FOURIER_WIDTH], a[:, FOURIER_WIDTH:]], axis=0).astype(BF16)
        y_ref[0, j] = jnp.dot(m_ref[j].astype(BF16), st, preferred_element_type=F32)


def _fft2_kernel(yr_ref, yi_ref, cs_ref, o_ref):
    cs = cs_ref[...].astype(BF16)
    for j in range(FFT_SUB):
        st = jnp.concatenate([yr_ref[0, :, j, :], yi_ref[0, :, j, :]], axis=0).astype(BF16)
        o_ref[0, :, j, :] = jnp.dot(cs, st, preferred_element_type=F32)


def _ctx_fft_kernel(z_ref, fc_ref, ct_ref, o_ref):
    a = jnp.dot(z_ref[...].astype(BF16), fc_ref[...].astype(BF16), preferred_element_type=F32)
    st = jnp.concatenate([a[:, :FOURIER_WIDTH], a[:, FOURIER_WIDTH:]], axis=0).astype(BF16)
    o_ref[...] = jnp.dot(ct_ref[...].astype(BF16), st, preferred_element_type=F32)


def _fourier_spec(rows, z):
    B, T, C = rows.B, rows.T, rows.C
    n1 = T // FFT_N2
    fc, m1, cs2 = _fft_tables(T)
    zv = z.reshape(rows.n_tot // FFT_N2, FFT_N2, D)
    lane_blocks = D // FOURIER_WIDTH
    ybuf = pl.pallas_call(
        _fft1_kernel,
        grid=(B, FFT_N2 // FFT_SUB),
        in_specs=[pl.BlockSpec((n1, FFT_SUB, FOURIER_WIDTH), lambda b, j: (b, j, lane_blocks - 1)),
                  _resident((FOURIER_WIDTH, 2 * FOURIER_WIDTH)),
                  pl.BlockSpec((FFT_SUB, 2 * n1, 2 * n1), lambda b, j: (j, 0, 0))],
        out_specs=pl.BlockSpec((1, FFT_SUB, 2 * n1, FOURIER_WIDTH), lambda b, j: (b, j, 0, 0)),
        out_shape=jax.ShapeDtypeStruct((B, FFT_N2, 2 * n1, FOURIER_WIDTH), F32),
        compiler_params=_cparams(("arbitrary", "arbitrary")),
        name="fft_stage1",
    )(zv, fc, m1)
    nblk = n1 // FFT_SUB
    spec_lat = pl.pallas_call(
        _fft2_kernel,
        grid=(B, nblk),
        in_specs=[pl.BlockSpec((1, FFT_N2, FFT_SUB, FOURIER_WIDTH), lambda b, j: (b, 0, j, 0)),
                  pl.BlockSpec((1, FFT_N2, FFT_SUB, FOURIER_WIDTH), lambda b, j: (b, 0, j + nblk, 0)),
                  _resident((FFT_N2, 2 * FFT_N2))],
        out_specs=pl.BlockSpec((1, FFT_N2, FFT_SUB, FOURIER_WIDTH), lambda b, j: (b, 0, j, 0)),
        out_shape=jax.ShapeDtypeStruct((B, FFT_N2, n1, FOURIER_WIDTH), F32),
        compiler_params=_cparams(("arbitrary", "arbitrary")),
        name="fft_stage2",
    )(ybuf, ybuf, cs2)
    ct = _ctx_fft_table(C)
    spec_ctx = pl.pallas_call(
        _ctx_fft_kernel,
        grid=(B,),
        in_specs=[pl.BlockSpec((C, FOURIER_WIDTH), lambda b: (rows.n_lat // C + b, lane_blocks - 1)),
                  _resident((FOURIER_WIDTH, 2 * FOURIER_WIDTH)),
                  _resident((C, 2 * C))],
        out_specs=pl.BlockSpec((C, FOURIER_WIDTH), lambda b: (b, 0)),
        out_shape=jax.ShapeDtypeStruct((B * C, FOURIER_WIDTH), F32),
        compiler_params=_cparams(("arbitrary",)),
        name="fft_ctx",
    )(z, fc, ct)
    return spec_lat.reshape(rows.n_lat, FOURIER_WIDTH), spec_ctx


def _ab_out_kernel(z_ref, zp_ref, zn_ref, spec_ref, spec_c_ref, h_ref, h_c_ref, mod_ref, pw_ref, ps_ref,
                   fw_ref, wo_ref, o_ref, *, rows):
    i = pl.program_id(0)
    pos0, seqlen = rows.seq_pos(i)
    n = TM + 2 * HALO
    ze = jnp.concatenate([zp_ref[...], z_ref[...], zn_ref[...]], axis=0)
    pe = pos0 - HALO + lax.broadcasted_iota(jnp.int32, (n, 1), 0)
    ze = jnp.where((pe >= 0) & (pe < seqlen), ze, 0.0)
    sums = {1: ze}
    w = 1
    while w < max(POOL_WINDOWS):
        sums[2 * w] = sums[w] + pltpu.roll(sums[w], n - w, 0)
        w *= 2
    pos = pos0 + lax.broadcasted_iota(jnp.int32, (TM, 1), 0)
    lane = lax.broadcasted_iota(jnp.int32, (1, POOL_WIDTH), 1)
    zc = ze[HALO:HALO + TM]
    mean = jnp.zeros((TM, POOL_WIDTH), F32)
    for g, w in enumerate(POOL_WINDOWS):
        left = w // 2
        right = w - 1 - left
        cw = pltpu.roll(sums[w], left, 0)[HALO:HALO + TM]
        cnt = (jnp.minimum(pos + right + 1, seqlen) - jnp.maximum(pos - left, 0)).astype(F32)
        in_group = (lane >= g * POOL_GROUP) & (lane < (g + 1) * POOL_GROUP)
        mean = jnp.where(in_group, cw / cnt, mean)
    pooled = (mean - zc).astype(BF16)
    ya = jnp.dot(pooled, pw_ref[...], preferred_element_type=F32) * ps_ref[...]
    yb = jnp.dot(_pick_rows(rows, spec_ref, spec_c_ref).astype(BF16), fw_ref[...], preferred_element_type=F32)
    y = (jnp.dot(ya.astype(BF16), wo_ref[0:POOL_WIDTH, :], preferred_element_type=F32)
         + jnp.dot(yb.astype(BF16), wo_ref[POOL_WIDTH:, :], preferred_element_type=F32))
    gate = mod_ref[0][:, 2 * D:3 * D]
    o_ref[...] = _pick_rows(rows, h_ref, h_c_ref) + gate * y


def _ab_out(rows, z, spec_lat, spec_ctx, x_lat, x_ctx, mod3, layer, pw_bd, pscale, fw, wo):
    n_rows = rows.n_tot
    zprev, znext = rows.halo_specs(POOL_WIDTH, n_rows)
    return pl.pallas_call(
        functools.partial(_ab_out_kernel, rows=rows),
        grid=(n_rows // TM,),
        in_specs=[pl.BlockSpec((TM, POOL_WIDTH), lambda i: (i, 0)), zprev, znext,
                  *_split_rows_specs(rows, FOURIER_WIDTH), *_split_rows_specs(rows, D),
                  pl.BlockSpec((1, 1, N_MOD * D), lambda i: (layer * 8 + rows.mod_row(i), 0, 0)),
                  _resident((POOL_WIDTH, POOL_WIDTH)), _resident((1, POOL_WIDTH)),
                  _resident((FOURIER_WIDTH, FOURIER_WIDTH)), _resident((D, D))],
        out_specs=pl.BlockSpec((TM, D), lambda i: (i, 0)),
        out_shape=jax.ShapeDtypeStruct((n_rows, D), F32),
        compiler_params=_cparams(("arbitrary",)),
        name="ab_out",
    )(z, z, z, spec_lat, spec_ctx, x_lat, x_ctx, mod3, pw_bd, pscale, fw, wo)


def _ffn_kernel(h_ref, hp_ref, hn_ref, g_ref, mod_ref, wu_ref, cw_ref, wd_ref, o_ref, up_ref, act_ref, *, rows):
    i = pl.program_id(0)
    pos0, seqlen = rows.seq_pos(i)
    n = TM + 2 * HALO
    m = mod_ref[0]
    he = jnp.concatenate([hp_ref[...], h_ref[...], hn_ref[...]], axis=0)
    u = _norm_mod(he, g_ref[...], m[:, 3 * D:4 * D], m[:, 4 * D:5 * D])
    pe = pos0 - HALO + lax.broadcasted_iota(jnp.int32, (n, 1), 0)
    u = jnp.where((pe >= 0) & (pe < seqlen), u, 0.0).astype(BF16)

    def conv(slot, col):
        w = cw_ref[:, col:col + FF_CHUNK]
        return (up_ref[slot, HALO - 1:HALO - 1 + TM, :] * w[0:1] + up_ref[slot, HALO:HALO + TM, :] * w[1:2]
                + up_ref[slot, HALO + 1:HALO + 1 + TM, :] * w[2:3])

    for c in range(D_FF // FF_CHUNK):
        lo = c * FF_CHUNK
        sg, sv = 2 * (c % 2), 2 * (c % 2) + 1
        up_ref[sg] = jnp.dot(u, wu_ref[:, lo:lo + FF_CHUNK], preferred_element_type=F32)
        up_ref[sv] = jnp.dot(u, wu_ref[:, D_FF + lo:D_FF + lo + FF_CHUNK], preferred_element_type=F32)
        gate = conv(sg, lo)
        val = conv(sv, D_FF + lo)
        half = 0.5 * gate
        act_ref[:, lo:lo + FF_CHUNK] = ((half + half * jnp.tanh(half)) * val).astype(BF16)
    acc = jnp.dot(act_ref[...], wd_ref[...], preferred_element_type=F32)
    o_ref[...] = h_ref[...] + m[:, 5 * D:6 * D] * acc


def _ffn(rows, h, g, mod3, layer, wu, cw, wd):
    n_rows = h.shape[0]
    hprev, hnext = rows.halo_specs(D, n_rows)
    return pl.pallas_call(
        functools.partial(_ffn_kernel, rows=rows),
        grid=(n_rows // TM,),
        in_specs=[pl.BlockSpec((TM, D), lambda i: (i, 0)), hprev, hnext,
                  _resident((1, D)),
                  pl.BlockSpec((1, 1, N_MOD * D), lambda i: (layer * 8 + rows.mod_row(i), 0, 0)),
                  _resident((D, 2 * D_FF)), _resident((3, 2 * D_FF)), _resident((D_FF, D))],
        out_specs=pl.BlockSpec((TM, D), lambda i: (i, 0)),
        out_shape=jax.ShapeDtypeStruct((n_rows, D), F32),
        scratch_shapes=[pltpu.VMEM((4, TM + 2 * HALO, FF_CHUNK), F32), pltpu.VMEM((TM, D_FF), BF16)],
        compiler_params=_cparams(("arbitrary",)),
        name="conv_ffn",
    )(h, h, h, g, mod3, wu, cw, wd)


def _rope_tables(T):
    half = HEAD_DIM // 2
    inv = ROPE_THETA ** (-np.arange(0, half, 2, dtype=np.float64) / half)
    t = np.arange(T)
    ang_r = (t // GRID_W)[:, None] * inv[None, :]
    ang_c = (t % GRID_W)[:, None] * inv[None, :]
    cos = np.concatenate([np.cos(ang_r), np.cos(ang_r), np.cos(ang_c), np.cos(ang_c)], axis=1)
    sin = np.concatenate([-np.sin(ang_r), np.sin(ang_r), -np.sin(ang_c), np.sin(ang_c)], axis=1)
    cos = np.concatenate([np.tile(cos, (1, 2)), np.ones((TM, 128))], axis=0)
    sin = np.concatenate([np.tile(sin, (1, 2)), np.zeros((TM, 128))], axis=0)
    return jnp.asarray(cos, F32), jnp.asarray(sin, F32)


def _rope128(x, cos, sin):
    lane = lax.broadcasted_iota(jnp.int32, (1, 128), 1)
    partner = jnp.where((lane & 31) < 16, pltpu.roll(x, 128 - 16, 1), pltpu.roll(x, 16, 1))
    return x * cos + partner * sin


def _cd_in_kernel(x_ref, g_ref, mod_ref, w_ref, cos_ref, sin_ref, qn_ref, kn_ref, ones_ref,
                  q_out, k_out, v_out, rkv_out, lora_out):
    m = mod_ref[0]
    u = _norm_mod(x_ref[...], g_ref[...], m[:, 0:D], m[:, D:2 * D]).astype(BF16)
    n_att, n_rkv, n_lora = CD_SPLITS
    rkv_out[...] = jnp.dot(u, w_ref[:, n_att:n_att + n_rkv], preferred_element_type=F32)
    lora_out[...] = jnp.dot(u, w_ref[:, n_att + n_rkv:], preferred_element_type=F32)
    p = jnp.dot(u, w_ref[:, 0:n_att], preferred_element_type=F32)
    cos, sin = cos_ref[...], sin_ref[...]
    q = p[:, 0:ATT_WIDTH]
    ms = _seg_sum(q * q, ones_ref[...]) * (1.0 / HEAD_DIM)
    q = q * lax.rsqrt(ms + EPS) * qn_ref[...]
    q = jnp.concatenate([_rope128(q[:, 128 * j:128 * (j + 1)], cos, sin) for j in range(4)], axis=1)
    q_out[...] = (q * HEAD_DIM ** -0.5).astype(BF16)
    k = p[:, ATT_WIDTH:ATT_WIDTH + KV_WIDTH]
    ms = _seg_sum(k * k, ones_ref[0:KV_WIDTH, 0:KV_WIDTH]) * (1.0 / HEAD_DIM)
    k = k * lax.rsqrt(ms + EPS) * kn_ref[...]
    k_out[...] = _rope128(k, cos, sin).astype(BF16)
    v_out[...] = p[:, ATT_WIDTH + KV_WIDTH:].astype(BF16)


def _cd_in(rows, h, g, mod3, layer, w_bf16, q_norm, k_norm, ones_bd):
    cos, sin = _rope_tables(rows.T)
    tps = rows.tiles_per_seq
    tab = lambda i: (jnp.where(i < rows.lat_tiles, lax.rem(i, tps), tps), 0)
    qn = jnp.tile(q_norm, N_Q_HEADS).reshape(1, ATT_WIDTH)
    kn = jnp.tile(k_norm, N_KV_HEADS).reshape(1, KV_WIDTH)
    widths = (ATT_WIDTH, KV_WIDTH, KV_WIDTH, CD_SPLITS[1], CD_SPLITS[2])
    dtypes = (BF16, BF16, BF16, F32, F32)
    return pl.pallas_call(
        _cd_in_kernel,
        grid=(rows.tot_tiles,),
        in_specs=[pl.BlockSpec((TM, D), lambda i: (i, 0)),
                  _resident((1, D)),
                  pl.BlockSpec((1, 1, N_MOD * D), lambda i: (layer * 8 + rows.mod_row(i), 0, 0)),
                  _resident((D, sum(CD_SPLITS))),
                  pl.BlockSpec((TM, 128), tab), pl.BlockSpec((TM, 128), tab),
                  _resident((1, ATT_WIDTH)), _resident((1, KV_WIDTH)),
                  _resident((RWKV_WIDTH, RWKV_WIDTH))],
        out_specs=[pl.BlockSpec((TM, n), lambda i: (i, 0)) for n in widths],
        out_shape=[jax.ShapeDtypeStruct((rows.n_tot, n), dt) for n, dt in zip(widths, dtypes)],
        compiler_params=_cparams(("arbitrary",)),
        name="cd_in",
    )(h, g, mod3, w_bf16, cos, sin, qn, kn, ones_bd)


def _attn_bias(C):
    Q = ATT_BLOCK
    ql = (np.arange(GQA_GROUP * Q) % Q)[:, None]
    col = np.arange(3 * Q + C)[None, :]
    rel = col - ql
    band = (rel >= 0) & (rel <= 2 * Q) & (col < 3 * Q)
    is_ctx = np.broadcast_to(col >= 3 * Q, band.shape)
    variants = [band & (col >= Q) | is_ctx, band | is_ctx, band & (col < 2 * Q) | is_ctx]
    return jnp.asarray(np.where(np.stack(variants), 0.0, -1e30), F32)


def _attn_kernel(q_ref, kp_ref, kc_ref, kn_ref, kx_ref, vp_ref, vc_ref, vn_ref, vx_ref, sink_ref, bias_ref,
                 o_ref, *, nq):
    i = pl.program_id(1)
    Q = ATT_BLOCK
    heads = range(N_KV_HEADS)
    k_all = jnp.concatenate([kp_ref[...], kc_ref[...], kn_ref[...], kx_ref[...]], axis=0)
    v_all = jnp.concatenate([vp_ref[...], vc_ref[...], vn_ref[...], vx_ref[...]], axis=0)
    bias = bias_ref[jnp.where(i == 0, 0, jnp.where(i == nq - 1, 2, 1))]
    q = q_ref[...]
    q4 = [jnp.concatenate([q[:, HEAD_DIM * (GQA_GROUP * hk + g):HEAD_DIM * (GQA_GROUP * hk + g + 1)]
                           for g in range(GQA_GROUP)], axis=0) for hk in heads]
    kh = [k_all[:, HEAD_DIM * hk:HEAD_DIM * (hk + 1)] for hk in heads]
    vh = [v_all[:, HEAD_DIM * hk:HEAD_DIM * (hk + 1)] for hk in heads]
    s = [lax.dot_general(q4[hk], kh[hk], (((1,), (1,)), ((), ())), preferred_element_type=F32) + bias
         for hk in heads]
    sk = [sink_ref[hk][:, 0:1] for hk in heads]
    m = [jnp.maximum(jnp.max(s[hk], axis=1, keepdims=True), sk[hk]) for hk in heads]
    p = [jnp.exp(s[hk] - m[hk]) for hk in heads]
    den = [jnp.sum(p[hk], axis=1, keepdims=True) + jnp.exp(sk[hk] - m[hk]) for hk in heads]
    o = [jnp.dot(p[hk].astype(BF16), vh[hk], preferred_element_type=F32) * (1.0 / den[hk]) for hk in heads]
    o_ref[...] = jnp.concatenate([o[hk][Q * g:Q * (g + 1)] for hk in heads for g in range(GQA_GROUP)],
                                 axis=1).astype(BF16)


def _attention(rows, qr, kr, vr, sink):
    B, T, C = rows.B, rows.T, rows.C
    nq = T // ATT_BLOCK
    assert nq >= 2
    blk = lambda off: pl.BlockSpec(
        (ATT_BLOCK, KV_WIDTH), lambda b, i: (b * nq + jnp.clip(i + off, 0, nq - 1), 0))
    ctx = pl.BlockSpec((C, KV_WIDTH), lambda b, i: (rows.n_lat // C + b, 0))
    sink_col = jnp.repeat(sink.reshape(N_KV_HEADS, GQA_GROUP), ATT_BLOCK, axis=1)
    sink_col = jnp.broadcast_to(sink_col[:, :, None], (N_KV_HEADS, GQA_GROUP * ATT_BLOCK, 128))
    return pl.pallas_call(
        functools.partial(_attn_kernel, nq=nq),
        grid=(B, nq),
        in_specs=[pl.BlockSpec((ATT_BLOCK, ATT_WIDTH), lambda b, i: (b * nq + i, 0)),
                  blk(-1), blk(0), blk(1), ctx, blk(-1), blk(0), blk(1), ctx,
                  _resident((N_KV_HEADS, GQA_GROUP * ATT_BLOCK, 128)),
                  _resident((3, GQA_GROUP * ATT_BLOCK, 3 * ATT_BLOCK + C))],
        out_specs=pl.BlockSpec((ATT_BLOCK, ATT_WIDTH), lambda b, i: (b * nq + i, 0)),
        out_shape=jax.ShapeDtypeStruct((rows.n_lat, ATT_WIDTH), BF16),
        compiler_params=_cparams(("arbitrary", "arbitrary")),
        name="window_attention",
    )(qr, kr, kr, kr, kr, vr, vr, vr, vr, sink_col, _attn_bias(C))


def _rwkv_prep_kernel(x_ref, xp_ref, xn_ref, lora_ref, cw_ref, w0_ref, w2_ref, a0_ref, a2_ref,
                      kk_ref, ka_ref, rk_ref, ones_ref,
                      r_out, v_out, kkn_out, w_out0, kka_out0, km_out0, w_out1, kka_out1, km_out1, bonus_out,
                      *, rows):
    i = pl.program_id(0)
    pos0, seqlen = rows.seq_pos(i)
    xe = jnp.concatenate([xp_ref[...], x_ref[...], xn_ref[...]], axis=0)
    rkv = _dwconv3_ext(xe, cw_ref[...], pos0, seqlen)
    W = RWKV_WIDTH
    r, k, v = rkv[:, 0:W], rkv[:, W:2 * W], rkv[:, 2 * W:3 * W]
    kk = k * kk_ref[...]
    kk = kk * lax.rsqrt(_seg_sum(kk * kk, ones_ref[...]) + EPS)
    r_out[...] = r
    v_out[...] = v
    kkn_out[...] = kk
    lora = lora_ref[...]
    wl = jnp.tanh(lora[:, 0:64]).astype(BF16)
    al = lora[:, 64:128].astype(BF16)
    outs = ((w_out0, kka_out0, km_out0), (w_out1, kka_out1, km_out1))
    bonus = jnp.zeros_like(r)
    for d in range(2):
        x = -(w0_ref[d:d + 1, :] + jnp.dot(wl, w2_ref[d], preferred_element_type=F32))
        softplus = jnp.maximum(x, 0.0) + jnp.log(1.0 + jnp.exp(-jnp.abs(x)))
        w_log = -softplus - 0.5
        a = _sigmoid(a0_ref[d:d + 1, :] + jnp.dot(al, a2_ref[d], preferred_element_type=F32))
        w_o, kka_o, km_o = outs[d]
        w_o[...] = -jnp.exp(w_log)
        kka_o[...] = kk * a
        km = k * (1.0 + (a - 1.0) * ka_ref[...])
        km_o[...] = km
        bonus = bonus + _seg_sum(r * km * rk_ref[...], ones_ref[...])
    bonus_out[...] = bonus * v


def _rwkv_prep(rows, p_rkv, p_lora, conv_w, w0, w2, a0, a2, k_k, k_a, r_k, ones_bd):
    n_rows = rows.n_tot
    W = RWKV_WIDTH
    xprev, xnext = rows.halo_specs(3 * W, n_rows)
    out_spec = pl.BlockSpec((TM, W), lambda i: (i, 0))
    return pl.pallas_call(
        functools.partial(_rwkv_prep_kernel, rows=rows),
        grid=(rows.tot_tiles,),
        in_specs=[pl.BlockSpec((TM, 3 * W), lambda i: (i, 0)), xprev, xnext,
                  pl.BlockSpec((TM, CD_SPLITS[2]), lambda i: (i, 0)),
                  _resident((3, 3 * W)), _resident((2, W)), _resident((2, 64, W)),
                  _resident((2, W)), _resident((2, 64, W)), _resident((1, W)), _resident((1, W)),
                  _resident((1, W)), _resident((W, W))],
        out_specs=[out_spec] * 10,
        out_shape=[jax.ShapeDtypeStruct((n_rows, W), F32)] * 10,
        compiler_params=_cparams(("arbitrary",)),
        name="rwkv_prep",
    )(p_rkv, p_rkv, p_rkv, p_lora, conv_w, w0, w2, a0, a2, k_k, k_a, r_k, ones_bd)


def _bd(x, bd_mask):
    xb = x.astype(BF16)
    return jnp.where(bd_mask, jnp.concatenate([xb] * SCAN_HEADS, axis=0), jnp.zeros((), BF16))


def _mm(a, b):
    return jnp.dot(a.astype(BF16), b, preferred_element_type=F32)


def _diag_blocks(f):
    lane_head = lax.broadcasted_iota(jnp.int32, (SCAN_L, 256), 1) >> 6
    out = jnp.zeros((SCAN_L, 256), F32)
    for h in range(SCAN_HEADS):
        out = jnp.where(lane_head == h, f[SCAN_L * h:SCAN_L * (h + 1)], out)
    return out


def _chunk_step(chains, states, reverse):
    L = SCAN_L
    n = len(chains)
    row = lax.broadcasted_iota(jnp.int32, (L, 256), 0)
    idx = lax.broadcasted_iota(jnp.int32, (L, 256), 1) & (L - 1)
    incl = (idx >= row) if reverse else (idx <= row)
    strict = (idx > row) if reverse else (idx < row)
    eye = (idx == row).astype(F32)
    bd_mask = ((lax.broadcasted_iota(jnp.int32, (256, 256), 0) >> 6)
               == (lax.broadcasted_iota(jnp.int32, (256, 256), 1) >> 6))
    r64 = lax.broadcasted_iota(jnp.int32, (L, L), 0)
    c64 = lax.broadcasted_iota(jnp.int32, (L, L), 1)
    tri = ((c64 >= r64) if reverse else (c64 <= r64)).astype(BF16)
    nt = (((1,), (1,)), ((), ()))
    tn = (((0,), (0,)), ((), ()))
    bd = lambda x: _bd(x, bd_mask)
    last = 0 if reverse else L - 1

    def cum_log_decay(lw):
        h1 = lw.astype(BF16)
        r1 = lw - h1.astype(F32)
        h2 = r1.astype(BF16)
        h3 = (r1 - h2.astype(F32)).astype(BF16)
        return (jnp.dot(tri, h1, preferred_element_type=F32) + jnp.dot(tri, h2, preferred_element_type=F32)
                + jnp.dot(tri, h3, preferred_element_type=F32))

    lcs = [cum_log_decay(c[3]) for c in chains]
    pre = []
    for (r, v, kk, lw, kka, km), lc in zip(chains, lcs):
        e_in = jnp.exp(lc)
        e_neg = jnp.exp(-lc)
        p_end = e_in[last:last + 1]
        bb = kka * e_neg
        kb = km * e_neg
        pre.append(dict(ab=-kk * jnp.exp(lc - lw), bb=bb, kb=kb, rb=r * e_in, bh=bb * p_end, kh=kb * p_end,
                        p_end=p_end, v=v))
    lhs = [jnp.concatenate([p["ab"], p["rb"]], axis=0).astype(BF16) for p in pre]
    g1 = [lax.dot_general(l, bd(p["bb"]), nt, preferred_element_type=F32) for l, p in zip(lhs, pre)]
    g2 = [lax.dot_general(l, bd(p["kb"]), nt, preferred_element_type=F32) for l, p in zip(lhs, pre)]
    aab = [jnp.where(strict, g[:L], 0.0) for g in g1]
    brb = [jnp.where(incl, g[L:], 0.0) for g in g1]
    aak = [jnp.where(strict, g[:L], 0.0) for g in g2]
    brk = [jnp.where(incl, g[L:], 0.0) for g in g2]

    def coupling(m):
        sh = m.bit_length() - 1
        tb, sb = row >> sh, idx >> sh
        later, earlier = (sb, tb) if reverse else (tb, sb)
        return ((row >> (sh + 1)) == (idx >> (sh + 1))) & ((later & 1) == 1) & ((earlier & 1) == 0)

    cm = coupling(1)
    tw = [eye + jnp.where(cm, a, 0.0) for a in aab]
    m = 2
    while m < L:
        cm = coupling(m)
        x = [_mm(t, bd(jnp.where(cm, a, 0.0))) for t, a in zip(tw, aab)]
        tw = [t + _mm(xi, bd(t)) for t, xi in zip(tw, x)]
        m *= 2

    v_bd = [bd(p["v"]) for p in pre]
    av = [_mm(a, vb) for a, vb in zip(aak, v_bd)]
    wu = [_mm(t, jnp.concatenate([bd(p["ab"]), bd(a)], axis=1)) for t, p, a in zip(tw, pre, av)]
    qy = [_mm(b, jnp.concatenate([bd(w[:, :256]), bd(w[:, 256:])], axis=1)) for b, w in zip(brb, wu)]
    y2b = [_mm(b, vb) for b, vb in zip(brk, v_bd)]
    f1 = [lax.dot_general(p["bh"].astype(BF16), w.astype(BF16), tn, preferred_element_type=F32)
          for p, w in zip(pre, wu)]
    f2 = [lax.dot_general(p["kh"].astype(BF16), p["v"].astype(BF16), tn, preferred_element_type=F32)
          for p in pre]
    ys, new_states = [], []
    for i in range(n):
        q = pre[i]["rb"] + qy[i][:, :256]
        y2 = qy[i][:, 256:] + y2b[i]
        mw = eye * pre[i]["p_end"] + _diag_blocks(f1[i][:, :256])
        nw = _diag_blocks(f1[i][:, 256:]) + _diag_blocks(f2[i])
        qm = jnp.concatenate([q, mw], axis=0).astype(BF16)
        z = jnp.dot(qm, bd(states[i]), preferred_element_type=F32)
        ys.append(z[:L] + y2)
        new_states.append(z[L:] + nw)
    return ys, new_states


def _chunk_scan_kernel(*refs, reverse, n_batch):
    ins = [refs[6 * b:6 * (b + 1)] for b in range(n_batch)]
    y_ref = refs[6 * n_batch]
    s_ref = refs[6 * n_batch + 1]

    @pl.when(pl.program_id(0) == 0)
    def _():
        s_ref[...] = jnp.zeros_like(s_ref)

    groups = [(b, slice(256 * g, 256 * (g + 1))) for b in range(n_batch)
              for g in range(RWKV_HEADS // SCAN_HEADS)]
    chains = [tuple(ref[:, sl] for ref in ins[b]) for b, sl in groups]
    states = [s_ref[b, :, sl] for b, sl in groups]
    ys, new_states = _chunk_step(chains, states, reverse)
    for (b, sl), y, s_new in zip(groups, ys, new_states):
        y_ref[b, :, sl] = y
        s_ref[b, :, sl] = s_new


def _rwkv_scan_dir(rows, r, v, kk, lw, kka, km, reverse):
    B, T, C = rows.B, rows.T, rows.C
    L = SCAN_L
    nctx, nlat = C // L, T // L

    def in_blk(c, b):
        j_ctx = (nctx - 1 - c) if reverse else c
        j_lat = (nlat - 1 - (c - nctx)) if reverse else (c - nctx)
        return (jnp.where(c < nctx, rows.n_lat // L + b * nctx + j_ctx, b * nlat + j_lat), 0)

    def out_blk(c):
        j = jnp.maximum(c - nctx, 0)
        return (0, (nlat - 1 - j) if reverse else j, 0)

    in_specs, args = [], []
    for b in range(B):
        in_specs += [pl.BlockSpec((L, RWKV_WIDTH), functools.partial(in_blk, b=b))] * 6
        args += [r, v, kk, lw, kka, km]
    y = pl.pallas_call(
        functools.partial(_chunk_scan_kernel, reverse=reverse, n_batch=B),
        grid=(nctx + nlat,),
        in_specs=in_specs,
        out_specs=pl.BlockSpec((B, L, RWKV_WIDTH), out_blk),
        out_shape=jax.ShapeDtypeStruct((B, T, RWKV_WIDTH), F32),
        scratch_shapes=[pltpu.VMEM((B, HEAD_DIM, RWKV_WIDTH), F32)],
        compiler_params=_cparams(("arbitrary",)),
        name="rwkv_scan_rev" if reverse else "rwkv_scan_fwd",
    )(*args)
    return y.reshape(rows.n_lat, RWKV_WIDTH)


def _cd_out_kernel(y0_ref, y1_ref, bonus_ref, lora_ref, att_ref, h_ref, mod_ref,
                   lw_ref, lb_ref, gu_ref, wo_ref, ones_ref, o_ref):
    ones_bd = ones_ref[...]
    inv = 1.0 / HEAD_DIM
    y = y0_ref[...] + y1_ref[...]
    mu = _seg_sum(y, ones_bd) * inv
    yc = y - mu
    var = _seg_sum(yc * yc, ones_bd) * inv
    yn = yc * lax.rsqrt(var + GN_EPS) * lw_ref[...] + lb_ref[...]
    gate = jnp.dot(_sigmoid(lora_ref[...][:, 128:256]).astype(BF16), gu_ref[...], preferred_element_type=F32)
    mix = ((yn + bonus_ref[...]) * gate).astype(BF16)
    out = (jnp.dot(att_ref[...], wo_ref[0:ATT_WIDTH, :], preferred_element_type=F32)
           + jnp.dot(mix, wo_ref[ATT_WIDTH:, :], preferred_element_type=F32))
    o_ref[...] = h_ref[...] + mod_ref[0][:, 2 * D:3 * D] * out


def _cd_out(rows, y0, y1, bonus, p_lora, att, h, mod3, layer, lnx_w, lnx_b, g_up, wo, ones_bd):
    W = RWKV_WIDTH
    tile = lambda n: pl.BlockSpec((TM, n), lambda i: (i, 0))
    return pl.pallas_call(
        _cd_out_kernel,
        grid=(rows.lat_tiles,),
        in_specs=[tile(W)] * 3 + [tile(CD_SPLITS[2]), tile(ATT_WIDTH), tile(D),
                  pl.BlockSpec((1, 1, N_MOD * D), lambda i: (layer * 8 + rows.mod_row(i), 0, 0)),
                  _resident((1, W)), _resident((1, W)),
                  _resident((128, W)), _resident((D, D)), _resident((W, W))],
        out_specs=tile(D),
        out_shape=jax.ShapeDtypeStruct((rows.n_lat, D), F32),
        compiler_params=_cparams(("arbitrary",)),
        name="cd_out",
    )(y0, y1, bonus, p_lora, att, h, mod3, lnx_w, lnx_b, g_up, wo, ones_bd)


def _block_diag(blocks):
    n = blocks.shape[0]
    g = blocks.shape[1]
    out = jnp.zeros((n * g, n * g), blocks.dtype)
    for j in range(n):
        out = out.at[j * g:(j + 1) * g, j * g:(j + 1) * g].set(blocks[j])
    return out


def kernel(x, c, ctx, c_ctx, ada_w, ada_b, norm1, norm2, ffn_up, ffn_conv, ffn_down, ab_w_in, pool_w,
           pool_scale, fourier_w, ab_w_out, cd_w_in, q_norm, k_norm, attn_sink, rwkv_conv, rwkv_w0,
           rwkv_w2, rwkv_a0, rwkv_a2, rwkv_k_k, rwkv_k_a, rwkv_r_k, rwkv_lnx_w, rwkv_lnx_b,
           rwkv_g_up, cd_w_out):
    B, T, _ = x.shape
    C = ctx.shape[1]
    depth = ada_w.shape[0]
    assert depth == 2 and B + 1 <= 8
    rows = _Rows(B, T, C)
    W = RWKV_WIDTH

    cc = jnp.zeros((8, D), F32).at[:B].set(c).at[B].set(c_ctx)
    mod3 = _modulation(cc, ada_w, ada_b).reshape(depth * 8, 1, N_MOD * D)
    ones_bd = _block_diag(jnp.ones((RWKV_HEADS, HEAD_DIM, HEAD_DIM), BF16))

    x_lat, x_ctx = x.reshape(B * T, D), ctx.reshape(B * C, D)

    z = _in_proj(rows, x_lat, x_ctx, norm1[0].reshape(1, D), mod3, 0, ab_w_in[0].astype(BF16), "ab_in")
    spec_lat, spec_ctx = _fourier_spec(rows, z)
    h = _ab_out(rows, z, spec_lat, spec_ctx, x_lat, x_ctx, mod3, 0, _block_diag(pool_w[0]).astype(BF16),
                pool_scale[0].reshape(1, POOL_WIDTH), fourier_w[0].astype(BF16), ab_w_out[0].astype(BF16))
    h = _ffn(rows, h, norm2[0].reshape(1, D), mod3, 0, ffn_up[0].astype(BF16), ffn_conv[0],
             ffn_down[0].astype(BF16))

    qr, kr, vr, p_rkv, p_lora = _cd_in(rows, h, norm1[1].reshape(1, D), mod3, 1, cd_w_in[0].astype(BF16),
                                       q_norm[0], k_norm[0], ones_bd)
    att = _attention(rows, qr, kr, vr, attn_sink[0])
    r, v, kk, w0, kka0, km0, w1, kka1, km1, bonus = _rwkv_prep(
        rows, p_rkv, p_lora, rwkv_conv[0], rwkv_w0[0], rwkv_w2[0].astype(BF16), rwkv_a0[0],
        rwkv_a2[0].astype(BF16), rwkv_k_k[0].reshape(1, W), rwkv_k_a[0].reshape(1, W),
        rwkv_r_k[0].reshape(1, W), ones_bd)
    y0 = _rwkv_scan_dir(rows, r, v, kk, w0, kka0, km0, False)
    y1 = _rwkv_scan_dir(rows, r, v, kk, w1, kka1, km1, True)
    h_lat = _cd_out(rows, y0, y1, bonus, p_lora, att, h, mod3, 1,
                    rwkv_lnx_w[0].reshape(1, W), rwkv_lnx_b[0].reshape(1, W),
                    rwkv_g_up[0].astype(BF16), cd_w_out[0].astype(BF16), ones_bd)
    lat_rows = _Rows(B, T, C)
    h_lat = _ffn(lat_rows, h_lat, norm2[1].reshape(1, D), mod3, 1, ffn_up[1].astype(BF16), ffn_conv[1],
                 ffn_down[1].astype(BF16))
    return h_lat.reshape(B, T, D)
```

```python
import functools

import numpy as np
import jax
import jax.numpy as jnp
from jax import lax
from jax.experimental import pallas as pl
from jax.experimental.pallas import tpu as pltpu

F32 = jnp.float32
BF16 = jnp.bfloat16

D = 1024
N_MOD = 6
EPS = 1e-6
GRID_W = 64
POOL_WINDOWS = (2, 4, 8, 16)
POOL_GROUP = 192
POOL_WIDTH = 768
FOURIER_WIDTH = 256
HEAD_DIM = 64
N_Q_HEADS = 8
N_KV_HEADS = 2
GQA_GROUP = 4
ATT_BLOCK = 128
ROPE_THETA = 10000.0
RWKV_HEADS = 8
RWKV_WIDTH = 512
GN_EPS = 64e-5
ATT_WIDTH = 512
KV_WIDTH = 128
D_FF = 2816
CD_SPLITS = (768, 1536, 256)

TM = 256
TM_WIDE = 512
HALO = 8
FF_CHUNK = 256
FFT_N2 = 64
FFT_SUB = 8
SCAN_L = 64
SCAN_HEADS = 4
VMEM_LIMIT = 56 * 1024 * 1024


def _cparams(sem):
    return pltpu.CompilerParams(dimension_semantics=sem, vmem_limit_bytes=VMEM_LIMIT)


def _resident(shape):
    nd = len(shape)
    return pl.BlockSpec(shape, lambda *_: (0,) * nd, pipeline_mode=pl.Buffered(1))


def _norm_mod(x, g, shift, scale):
    ms = jnp.mean(x * x, axis=-1, keepdims=True)
    y = x * lax.rsqrt(ms + EPS) * g
    return y * (1.0 + scale) + shift


def _seg_sum(x, ones_bd):
    hi = x.astype(BF16)
    lo = (x - hi.astype(F32)).astype(BF16)
    return (jnp.dot(hi, ones_bd, preferred_element_type=F32)
            + jnp.dot(lo, ones_bd, preferred_element_type=F32))


def _sigmoid(x):
    return 1.0 / (1.0 + jnp.exp(-x))


class _Rows:
    def __init__(self, B, T, C, tm=TM):
        assert T % tm == 0 and (B * C) % tm == 0 and (B * T) % C == 0
        self.tm = tm
        assert T & (T - 1) == 0 and C & (C - 1) == 0
        self.B, self.T, self.C = B, T, C
        self.n_lat = B * T
        self.n_tot = B * T + B * C
        self.lat_tiles = self.n_lat // tm
        self.tot_tiles = self.n_tot // tm
        self.tiles_per_seq = T // tm

    def mod_row(self, i):
        return jnp.where(i < self.lat_tiles, i // self.tiles_per_seq, self.B)

    def seq_pos(self, i):
        is_lat = i < self.lat_tiles
        seqlen = jnp.where(is_lat, self.T, self.C)
        row0 = i * TM - jnp.where(is_lat, 0, self.n_lat)
        return lax.rem(row0, seqlen), seqlen

    def halo_specs(self, width, n_rows, col=0):
        blocks = n_rows // HALO
        per = TM // HALO
        prev = pl.BlockSpec((HALO, width), lambda i: (jnp.maximum(i * per - 1, 0), col))
        nxt = pl.BlockSpec((HALO, width), lambda i: (jnp.minimum((i + 1) * per, blocks - 1), col))
        return prev, nxt


def _dwconv3_ext(xe, w, pos0, seqlen):
    n = TM + 2 * HALO
    pos = pos0 + lax.broadcasted_iota(jnp.int32, (TM, 1), 0)
    xm = pltpu.roll(xe, 1, 0)[HALO:HALO + TM]
    xc = xe[HALO:HALO + TM]
    xn = pltpu.roll(xe, n - 1, 0)[HALO:HALO + TM]
    xm = jnp.where(pos == 0, 0.0, xm)
    xn = jnp.where(pos == seqlen - 1, 0.0, xn)
    return xm * w[0:1] + xc * w[1:2] + xn * w[2:3]


def _mod_kernel(cc_ref, w_ref, b_ref, o_ref):
    x = cc_ref[...]
    a = (x * _sigmoid(x)).astype(BF16)
    o_ref[0] = jnp.dot(a, w_ref[0].astype(BF16), preferred_element_type=F32) + b_ref[0]


def _modulation(cc, ada_w, ada_b):
    depth = ada_w.shape[0]
    nb = N_MOD * D // 1024
    return pl.pallas_call(
        _mod_kernel,
        grid=(depth, nb),
        in_specs=[pl.BlockSpec((8, D), lambda l, j: (0, 0)),
                  pl.BlockSpec((1, D, 1024), lambda l, j: (l, 0, j)),
                  pl.BlockSpec((1, 1, 1024), lambda l, j: (l, 0, j))],
        out_specs=pl.BlockSpec((1, 8, 1024), lambda l, j: (l, 0, j)),
        out_shape=jax.ShapeDtypeStruct((depth, 8, N_MOD * D), F32),
        compiler_params=_cparams(("arbitrary", "arbitrary")),
        name="adaln_mod",
    )(cc, ada_w, ada_b.reshape(depth, 1, N_MOD * D))


def _split_rows_specs(rows, width):
    lat = pl.BlockSpec((rows.tm, width), lambda i: (jnp.minimum(i, rows.lat_tiles - 1), 0))
    ctx = pl.BlockSpec((rows.tm, width), lambda i: (jnp.maximum(i - rows.lat_tiles, 0), 0))
    return lat, ctx


def _pick_rows(rows, lat_ref, ctx_ref):
    return jnp.where(pl.program_id(0) < rows.lat_tiles, lat_ref[...], ctx_ref[...])


def _in_kernel(x_ref, c_ref, g_ref, mod_ref, w_ref, o_ref, *, rows):
    m = mod_ref[0]
    u = _norm_mod(_pick_rows(rows, x_ref, c_ref), g_ref[...], m[:, 0:D], m[:, D:2 * D]).astype(BF16)
    o_ref[...] = jnp.dot(u, w_ref[...], preferred_element_type=F32)


def _in_proj(rows, x_lat, x_ctx, g, mod3, layer, w_bf16, name):
    width = w_bf16.shape[1]
    lat_spec, ctx_spec = _split_rows_specs(rows, D)
    return pl.pallas_call(
        functools.partial(_in_kernel, rows=rows),
        grid=(rows.tot_tiles,),
        in_specs=[lat_spec, ctx_spec,
                  _resident((1, D)),
                  pl.BlockSpec((1, 1, N_MOD * D), lambda i: (layer * 8 + rows.mod_row(i), 0, 0)),
                  _resident((D, width))],
        out_specs=pl.BlockSpec((rows.tm, width), lambda i: (i, 0)),
        out_shape=jax.ShapeDtypeStruct((rows.n_tot, width), F32),
        compiler_params=_cparams(("arbitrary",)),
        name=name,
    )(x_lat, x_ctx, g, mod3, w_bf16)


def _fft_tables(T):
    n1 = T // FFT_N2
    c = np.arange(FOURIER_WIDTH)
    ang = 2.0 * np.pi * (np.outer(c, c) % FOURIER_WIDTH) / FOURIER_WIDTH
    fc = np.concatenate([np.cos(ang), -np.sin(ang)], axis=1) / np.sqrt(FOURIER_WIDTH)
    t1 = np.arange(n1)[None, None, :]
    k1 = np.arange(n1)[None, :, None]
    t2 = np.arange(FFT_N2)[:, None, None]
    ang = 2.0 * np.pi * (((FFT_N2 * t1 + t2) * k1) % T) / T
    cm, sm = np.cos(ang) / np.sqrt(n1), np.sin(ang) / np.sqrt(n1)
    m1 = np.concatenate([np.concatenate([cm, sm], axis=2),
                         np.concatenate([-sm, cm], axis=2)], axis=1)
    k2 = np.arange(FFT_N2)
    ang = 2.0 * np.pi * (np.outer(k2, k2) % FFT_N2) / FFT_N2
    cs2 = np.concatenate([np.cos(ang), np.sin(ang)], axis=1) / np.sqrt(FFT_N2)
    return (jnp.asarray(fc, F32), jnp.asarray(m1, F32), jnp.asarray(cs2, F32))


def _ctx_fft_table(C):
    t = np.arange(C)
    ang = 2.0 * np.pi * (np.outer(t, t) % C) / C
    return jnp.asarray(np.concatenate([np.cos(ang), np.sin(ang)], axis=1) / np.sqrt(C), F32)


def _fft1_kernel(z_ref, fc_ref, m_ref, y_ref):
    fc = fc_ref[...].astype(BF16)
    for j in range(FFT_SUB):
        a = jnp.dot(z_ref[:, j, :].astype(BF16), fc, preferred_element_type=F32)
        st = jnp.concatenate([a[:, :FOURIER_WIDTH], a[:, FOURIER_WIDTH:]], axis=0).astype(BF16)
        y_ref[0, j] = jnp.dot(m_ref[j].astype(BF16), st, preferred_element_type=F32)


def _fft2_kernel(yr_ref, yi_ref, cs_ref, o_ref):
    cs = cs_ref[...].astype(BF16)
    for j in range(FFT_SUB):
        st = jnp.concatenate([yr_ref[0, :, j, :], yi_ref[0, :, j, :]], axis=0).astype(BF16)
        o_ref[0, :, j, :] = jnp.dot(cs, st, preferred_element_type=F32)


def _ctx_fft_kernel(z_ref, fc_ref, ct_ref, o_ref):
    a = jnp.dot(z_ref[...].astype(BF16), fc_ref[...].astype(BF16), preferred_element_type=F32)
    st = jnp.concatenate([a[:, :FOURIER_WIDTH], a[:, FOURIER_WIDTH:]], axis=0).astype(BF16)
    o_ref[...] = jnp.dot(ct_ref[...].astype(BF16), st, preferred_element_type=F32)


def _fourier_spec(rows, z):
    B, T, C = rows.B, rows.T, rows.C
    n1 = T // FFT_N2
    fc, m1, cs2 = _fft_tables(T)
    zv = z.reshape(rows.n_tot // FFT_N2, FFT_N2, D)
    lane_blocks = D // FOURIER_WIDTH
    ybuf = pl.pallas_call(
        _fft1_kernel,
        grid=(B, FFT_N2 // FFT_SUB),
        in_specs=[pl.BlockSpec((n1, FFT_SUB, FOURIER_WIDTH), lambda b, j: (b, j, lane_blocks - 1)),
                  _resident((FOURIER_WIDTH, 2 * FOURIER_WIDTH)),
                  pl.BlockSpec((FFT_SUB, 2 * n1, 2 * n1), lambda b, j: (j, 0, 0))],
        out_specs=pl.BlockSpec((1, FFT_SUB, 2 * n1, FOURIER_WIDTH), lambda b, j: (b, j, 0, 0)),
        out_shape=jax.ShapeDtypeStruct((B, FFT_N2, 2 * n1, FOURIER_WIDTH), F32),
        compiler_params=_cparams(("arbitrary", "arbitrary")),
        name="fft_stage1",
    )(zv, fc, m1)
    nblk = n1 // FFT_SUB
    spec_lat = pl.pallas_call(
        _fft2_kernel,
        grid=(B, nblk),
        in_specs=[pl.BlockSpec((1, FFT_N2, FFT_SUB, FOURIER_WIDTH), lambda b, j: (b, 0, j, 0)),
                  pl.BlockSpec((1, FFT_N2, FFT_SUB, FOURIER_WIDTH), lambda b, j: (b, 0, j + nblk, 0)),
                  _resident((FFT_N2, 2 * FFT_N2))],
        out_specs=pl.BlockSpec((1, FFT_N2, FFT_SUB, FOURIER_WIDTH), lambda b, j: (b, 0, j, 0)),
        out_shape=jax.ShapeDtypeStruct((B, FFT_N2, n1, FOURIER_WIDTH), F32),
        compiler_params=_cparams(("arbitrary", "arbitrary")),
        name="fft_stage2",
    )(ybuf, ybuf, cs2)
    ct = _ctx_fft_table(C)
    spec_ctx = pl.pallas_call(
        _ctx_fft_kernel,
        grid=(B,),
        in_specs=[pl.BlockSpec((C, FOURIER_WIDTH), lambda b: (rows.n_lat // C + b, lane_blocks - 1)),
                  _resident((FOURIER_WIDTH, 2 * FOURIER_WIDTH)),
                  _resident((C, 2 * C))],
        out_specs=pl.BlockSpec((C, FOURIER_WIDTH), lambda b: (b, 0)),
        out_shape=jax.ShapeDtypeStruct((B * C, FOURIER_WIDTH), F32),
        compiler_params=_cparams(("arbitrary",)),
        name="fft_ctx",
    )(z, fc, ct)
    return spec_lat.reshape(rows.n_lat, FOURIER_WIDTH), spec_ctx


def _ab_out_kernel(z_ref, zp_ref, zn_ref, spec_ref, spec_c_ref, h_ref, h_c_ref, mod_ref, pw_ref, ps_ref,
                   fw_ref, wo_ref, o_ref, *, rows):
    i = pl.program_id(0)
    pos0, seqlen = rows.seq_pos(i)
    n = TM + 2 * HALO
    ze = jnp.concatenate([zp_ref[...], z_ref[...], zn_ref[...]], axis=0)
    pe = pos0 - HALO + lax.broadcasted_iota(jnp.int32, (n, 1), 0)
    ze = jnp.where((pe >= 0) & (pe < seqlen), ze, 0.0)
    sums = {1: ze}
    w = 1
    while w < max(POOL_WINDOWS):
        sums[2 * w] = sums[w] + pltpu.roll(sums[w], n - w, 0)
        w *= 2
    pos = pos0 + lax.broadcasted_iota(jnp.int32, (TM, 1), 0)
    lane = lax.broadcasted_iota(jnp.int32, (1, POOL_WIDTH), 1)
    zc = ze[HALO:HALO + TM]
    mean = jnp.zeros((TM, POOL_WIDTH), F32)
    for g, w in enumerate(POOL_WINDOWS):
        left = w // 2
        right = w - 1 - left
        cw = pltpu.roll(sums[w], left, 0)[HALO:HALO + TM]
        cnt = (jnp.minimum(pos + right + 1, seqlen) - jnp.maximum(pos - left, 0)).astype(F32)
        in_group = (lane >= g * POOL_GROUP) & (lane < (g + 1) * POOL_GROUP)
        mean = jnp.where(in_group, cw / cnt, mean)
    pooled = (mean - zc).astype(BF16)
    ya = jnp.dot(pooled, pw_ref[...], preferred_element_type=F32) * ps_ref[...]
    yb = jnp.dot(_pick_rows(rows, spec_ref, spec_c_ref).astype(BF16), fw_ref[...], preferred_element_type=F32)
    y = (jnp.dot(ya.astype(BF16), wo_ref[0:POOL_WIDTH, :], preferred_element_type=F32)
         + jnp.dot(yb.astype(BF16), wo_ref[POOL_WIDTH:, :], preferred_element_type=F32))
    gate = mod_ref[0][:, 2 * D:3 * D]
    o_ref[...] = _pick_rows(rows, h_ref, h_c_ref) + gate * y


def _ab_out(rows, z, spec_lat, spec_ctx, x_lat, x_ctx, mod3, layer, pw_bd, pscale, fw, wo):
    n_rows = rows.n_tot
    zprev, znext = rows.halo_specs(POOL_WIDTH, n_rows)
    return pl.pallas_call(
        functools.partial(_ab_out_kernel, rows=rows),
        grid=(n_rows // TM,),
        in_specs=[pl.BlockSpec((TM, POOL_WIDTH), lambda i: (i, 0)), zprev, znext,
                  *_split_rows_specs(rows, FOURIER_WIDTH), *_split_rows_specs(rows, D),
                  pl.BlockSpec((1, 1, N_MOD * D), lambda i: (layer * 8 + rows.mod_row(i), 0, 0)),
                  _resident((POOL_WIDTH, POOL_WIDTH)), _resident((1, POOL_WIDTH)),
                  _resident((FOURIER_WIDTH, FOURIER_WIDTH)), _resident((D, D))],
        out_specs=pl.BlockSpec((TM, D), lambda i: (i, 0)),
        out_shape=jax.ShapeDtypeStruct((n_rows, D), F32),
        compiler_params=_cparams(("arbitrary",)),
        name="ab_out",
    )(z, z, z, spec_lat, spec_ctx, x_lat, x_ctx, mod3, pw_bd, pscale, fw, wo)


def _ffn_kernel(h_ref, hp_ref, hn_ref, g_ref, mod_ref, wu_ref, cw_ref, wd_ref, o_ref, up_ref, act_ref, *, rows):
    i = pl.program_id(0)
    pos0, seqlen = rows.seq_pos(i)
    n = TM + 2 * HALO
    m = mod_ref[0]
    he = jnp.concatenate([hp_ref[...], h_ref[...], hn_ref[...]], axis=0)
    u = _norm_mod(he, g_ref[...], m[:, 3 * D:4 * D], m[:, 4 * D:5 * D])
    pe = pos0 - HALO + lax.broadcasted_iota(jnp.int32, (n, 1), 0)
    u = jnp.where((pe >= 0) & (pe < seqlen), u, 0.0).astype(BF16)

    def conv(slot, col):
        w = cw_ref[:, col:col + FF_CHUNK]
        return (up_ref[slot, HALO - 1:HALO - 1 + TM, :] * w[0:1] + up_ref[slot, HALO:HALO + TM, :] * w[1:2]
                + up_ref[slot, HALO + 1:HALO + 1 + TM, :] * w[2:3])

    for c in range(D_FF // FF_CHUNK):
        lo = c * FF_CHUNK
        sg, sv = 2 * (c % 2), 2 * (c % 2) + 1
        up_ref[sg] = jnp.dot(u, wu_ref[:, lo:lo + FF_CHUNK], preferred_element_type=F32)
        up_ref[sv] = jnp.dot(u, wu_ref[:, D_FF + lo:D_FF + lo + FF_CHUNK], preferred_element_type=F32)
        gate = conv(sg, lo)
        val = conv(sv, D_FF + lo)
        half = 0.5 * gate
        act_ref[:, lo:lo + FF_CHUNK] = ((half + half * jnp.tanh(half)) * val).astype(BF16)
    acc = jnp.dot(act_ref[...], wd_ref[...], preferred_element_type=F32)
    o_ref[...] = h_ref[...] + m[:, 5 * D:6 * D] * acc


def _ffn(rows, h, g, mod3, layer, wu, cw, wd):
    n_rows = h.shape[0]
    hprev, hnext = rows.halo_specs(D, n_rows)
    return pl.pallas_call(
        functools.partial(_ffn_kernel, rows=rows),
        grid=(n_rows // TM,),
        in_specs=[pl.BlockSpec((TM, D), lambda i: (i, 0)), hprev, hnext,
                  _resident((1, D)),
                  pl.BlockSpec((1, 1, N_MOD * D), lambda i: (layer * 8 + rows.mod_row(i), 0, 0)),
                  _resident((D, 2 * D_FF)), _resident((3, 2 * D_FF)), _resident((D_FF, D))],
        out_specs=pl.BlockSpec((TM, D), lambda i: (i, 0)),
        out_shape=jax.ShapeDtypeStruct((n_rows, D), F32),
        scratch_shapes=[pltpu.VMEM((4, TM + 2 * HALO, FF_CHUNK), F32), pltpu.VMEM((TM, D_FF), BF16)],
        compiler_params=_cparams(("arbitrary",)),
        name="conv_ffn",
    )(h, h, h, g, mod3, wu, cw, wd)


def _rope_tables(T, tm):
    half = HEAD_DIM // 2
    inv = ROPE_THETA ** (-np.arange(0, half, 2, dtype=np.float64) / half)
    t = np.arange(T)
    ang_r = (t // GRID_W)[:, None] * inv[None, :]
    ang_c = (t % GRID_W)[:, None] * inv[None, :]
    cos = np.concatenate([np.cos(ang_r), np.cos(ang_r), np.cos(ang_c), np.cos(ang_c)], axis=1)
    sin = np.concatenate([-np.sin(ang_r), np.sin(ang_r), -np.sin(ang_c), np.sin(ang_c)], axis=1)
    cos = np.concatenate([np.tile(cos, (1, 2)), np.ones((tm, 128))], axis=0)
    sin = np.concatenate([np.tile(sin, (1, 2)), np.zeros((tm, 128))], axis=0)
    return jnp.asarray(cos, F32), jnp.asarray(sin, F32)


def _rope128(x, cos, sin):
    lane = lax.broadcasted_iota(jnp.int32, (1, 128), 1)
    partner = jnp.where((lane & 31) < 16, pltpu.roll(x, 128 - 16, 1), pltpu.roll(x, 16, 1))
    return x * cos + partner * sin


def _cd_in_kernel(x_ref, g_ref, mod_ref, w_ref, cos_ref, sin_ref, qn_ref, kn_ref, ones_ref,
                  q_out, k_out, v_out, rkv_out, lora_out):
    m = mod_ref[0]
    u = _norm_mod(x_ref[...], g_ref[...], m[:, 0:D], m[:, D:2 * D]).astype(BF16)
    n_att, n_rkv, n_lora = CD_SPLITS
    rkv_out[...] = jnp.dot(u, w_ref[:, n_att:n_att + n_rkv], preferred_element_type=F32)
    lora_out[...] = jnp.dot(u, w_ref[:, n_att + n_rkv:], preferred_element_type=F32)
    p = jnp.dot(u, w_ref[:, 0:n_att], preferred_element_type=F32)
    cos, sin = cos_ref[...], sin_ref[...]
    q = p[:, 0:ATT_WIDTH]
    ms = _seg_sum(q * q, ones_ref[...]) * (1.0 / HEAD_DIM)
    q = q * lax.rsqrt(ms + EPS) * qn_ref[...]
    q = jnp.concatenate([_rope128(q[:, 128 * j:128 * (j + 1)], cos, sin) for j in range(4)], axis=1)
    q_out[...] = (q * HEAD_DIM ** -0.5).astype(BF16)
    k = p[:, ATT_WIDTH:ATT_WIDTH + KV_WIDTH]
    ms = _seg_sum(k * k, ones_ref[0:KV_WIDTH, 0:KV_WIDTH]) * (1.0 / HEAD_DIM)
    k = k * lax.rsqrt(ms + EPS) * kn_ref[...]
    k_out[...] = _rope128(k, cos, sin).astype(BF16)
    v_out[...] = p[:, ATT_WIDTH + KV_WIDTH:].astype(BF16)


def _cd_in(rows, h, g, mod3, layer, w_bf16, q_norm, k_norm, ones_bd):
    cos, sin = _rope_tables(rows.T, rows.tm)
    tps = rows.tiles_per_seq
    tab = lambda i: (jnp.where(i < rows.lat_tiles, lax.rem(i, tps), tps), 0)
    qn = jnp.tile(q_norm, N_Q_HEADS).reshape(1, ATT_WIDTH)
    kn = jnp.tile(k_norm, N_KV_HEADS).reshape(1, KV_WIDTH)
    widths = (ATT_WIDTH, KV_WIDTH, KV_WIDTH, CD_SPLITS[1], CD_SPLITS[2])
    dtypes = (BF16, BF16, BF16, F32, F32)
    return pl.pallas_call(
        _cd_in_kernel,
        grid=(rows.tot_tiles,),
        in_specs=[pl.BlockSpec((rows.tm, D), lambda i: (i, 0)),
                  _resident((1, D)),
                  pl.BlockSpec((1, 1, N_MOD * D), lambda i: (layer * 8 + rows.mod_row(i), 0, 0)),
                  _resident((D, sum(CD_SPLITS))),
                  pl.BlockSpec((rows.tm, 128), tab), pl.BlockSpec((rows.tm, 128), tab),
                  _resident((1, ATT_WIDTH)), _resident((1, KV_WIDTH)),
                  _resident((RWKV_WIDTH, RWKV_WIDTH))],
        out_specs=[pl.BlockSpec((rows.tm, n), lambda i: (i, 0)) for n in widths],
        out_shape=[jax.ShapeDtypeStruct((rows.n_tot, n), dt) for n, dt in zip(widths, dtypes)],
        compiler_params=_cparams(("arbitrary",)),
        name="cd_in",
    )(h, g, mod3, w_bf16, cos, sin, qn, kn, ones_bd)


def _attn_bias(C):
    Q = ATT_BLOCK
    ql = (np.arange(GQA_GROUP * Q) % Q)[:, None]
    col = np.arange(3 * Q + C)[None, :]
    rel = col - ql
    band = (rel >= 0) & (rel <= 2 * Q) & (col < 3 * Q)
    is_ctx = np.broadcast_to(col >= 3 * Q, band.shape)
    variants = [band & (col >= Q) | is_ctx, band | is_ctx, band & (col < 2 * Q) | is_ctx]
    return jnp.asarray(np.where(np.stack(variants), 0.0, -1e30), F32)


def _attn_kernel(q_ref, kp_ref, kc_ref, kn_ref, kx_ref, vp_ref, vc_ref, vn_ref, vx_ref, sink_ref, bias_ref,
                 o_ref, *, nq):
    i = pl.program_id(1)
    Q = ATT_BLOCK
    heads = range(N_KV_HEADS)
    k_all = jnp.concatenate([kp_ref[...], kc_ref[...], kn_ref[...], kx_ref[...]], axis=0)
    v_all = jnp.concatenate([vp_ref[...], vc_ref[...], vn_ref[...], vx_ref[...]], axis=0)
    bias = bias_ref[jnp.where(i == 0, 0, jnp.where(i == nq - 1, 2, 1))]
    q = q_ref[...]
    q4 = [jnp.concatenate([q[:, HEAD_DIM * (GQA_GROUP * hk + g):HEAD_DIM * (GQA_GROUP * hk + g + 1)]
                           for g in range(GQA_GROUP)], axis=0) for hk in heads]
    kh = [k_all[:, HEAD_DIM * hk:HEAD_DIM * (hk + 1)] for hk in heads]
    vh = [v_all[:, HEAD_DIM * hk:HEAD_DIM * (hk + 1)] for hk in heads]
    s = [lax.dot_general(q4[hk], kh[hk], (((1,), (1,)), ((), ())), preferred_element_type=F32) + bias
         for hk in heads]
    sk = [sink_ref[hk][:, 0:1] for hk in heads]
    m = [jnp.maximum(jnp.max(s[hk], axis=1, keepdims=True), sk[hk]) for hk in heads]
    p = [jnp.exp(s[hk] - m[hk]) for hk in heads]
    den = [jnp.sum(p[hk], axis=1, keepdims=True) + jnp.exp(sk[hk] - m[hk]) for hk in heads]
    o = [jnp.dot(p[hk].astype(BF16), vh[hk], preferred_element_type=F32) * (1.0 / den[hk]) for hk in heads]
    o_ref[...] = jnp.concatenate([o[hk][Q * g:Q * (g + 1)] for hk in heads for g in range(GQA_GROUP)],
                                 axis=1).astype(BF16)


def _attention(rows, qr, kr, vr, sink):
    B, T, C = rows.B, rows.T, rows.C
    nq = T // ATT_BLOCK
    assert nq >= 2
    blk = lambda off: pl.BlockSpec(
        (ATT_BLOCK, KV_WIDTH), lambda b, i: (b * nq + jnp.clip(i + off, 0, nq - 1), 0))
    ctx = pl.BlockSpec((C, KV_WIDTH), lambda b, i: (rows.n_lat // C + b, 0))
    sink_col = jnp.repeat(sink.reshape(N_KV_HEADS, GQA_GROUP), ATT_BLOCK, axis=1)
    sink_col = jnp.broadcast_to(sink_col[:, :, None], (N_KV_HEADS, GQA_GROUP * ATT_BLOCK, 128))
    return pl.pallas_call(
        functools.partial(_attn_kernel, nq=nq),
        grid=(B, nq),
        in_specs=[pl.BlockSpec((ATT_BLOCK, ATT_WIDTH), lambda b, i: (b * nq + i, 0)),
                  blk(-1), blk(0), blk(1), ctx, blk(-1), blk(0), blk(1), ctx,
                  _resident((N_KV_HEADS, GQA_GROUP * ATT_BLOCK, 128)),
                  _resident((3, GQA_GROUP * ATT_BLOCK, 3 * ATT_BLOCK + C))],
        out_specs=pl.BlockSpec((ATT_BLOCK, ATT_WIDTH), lambda b, i: (b * nq + i, 0)),
        out_shape=jax.ShapeDtypeStruct((rows.n_lat, ATT_WIDTH), BF16),
        compiler_params=_cparams(("arbitrary", "arbitrary")),
        name="window_attention",
    )(qr, kr, kr, kr, kr, vr, vr, vr, vr, sink_col, _attn_bias(C))


def _rwkv_prep_kernel(x_ref, xp_ref, xn_ref, lora_ref, cw_ref, w0_ref, w2_ref, a0_ref, a2_ref,
                      kk_ref, ka_ref, rk_ref, ones_ref,
                      r_out, v_out, kkn_out, w_out0, kka_out0, km_out0, w_out1, kka_out1, km_out1, bonus_out,
                      *, rows):
    i = pl.program_id(0)
    pos0, seqlen = rows.seq_pos(i)
    xe = jnp.concatenate([xp_ref[...], x_ref[...], xn_ref[...]], axis=0)
    rkv = _dwconv3_ext(xe, cw_ref[...], pos0, seqlen)
    W = RWKV_WIDTH
    r, k, v = rkv[:, 0:W], rkv[:, W:2 * W], rkv[:, 2 * W:3 * W]
    kk = k * kk_ref[...]
    kk = kk * lax.rsqrt(_seg_sum(kk * kk, ones_ref[...]) + EPS)
    r_out[...] = r
    v_out[...] = v
    kkn_out[...] = kk
    lora = lora_ref[...]
    wl = jnp.tanh(lora[:, 0:64]).astype(BF16)
    al = lora[:, 64:128].astype(BF16)
    outs = ((w_out0, kka_out0, km_out0), (w_out1, kka_out1, km_out1))
    bonus = jnp.zeros_like(r)
    for d in range(2):
        x = -(w0_ref[d:d + 1, :] + jnp.dot(wl, w2_ref[d], preferred_element_type=F32))
        softplus = jnp.maximum(x, 0.0) + jnp.log(1.0 + jnp.exp(-jnp.abs(x)))
        w_log = -softplus - 0.5
        a = _sigmoid(a0_ref[d:d + 1, :] + jnp.dot(al, a2_ref[d], preferred_element_type=F32))
        w_o, kka_o, km_o = outs[d]
        w_o[...] = -jnp.exp(w_log)
        kka_o[...] = kk * a
        km = k * (1.0 + (a - 1.0) * ka_ref[...])
        km_o[...] = km
        bonus = bonus + km
    bonus_out[...] = _seg_sum(r * bonus * rk_ref[...], ones_ref[...]) * v


def _rwkv_prep(rows, p_rkv, p_lora, conv_w, w0, w2, a0, a2, k_k, k_a, r_k, ones_bd):
    n_rows = rows.n_tot
    W = RWKV_WIDTH
    xprev, xnext = rows.halo_specs(3 * W, n_rows)
    out_spec = pl.BlockSpec((TM, W), lambda i: (i, 0))
    return pl.pallas_call(
        functools.partial(_rwkv_prep_kernel, rows=rows),
        grid=(rows.tot_tiles,),
        in_specs=[pl.BlockSpec((TM, 3 * W), lambda i: (i, 0)), xprev, xnext,
                  pl.BlockSpec((TM, CD_SPLITS[2]), lambda i: (i, 0)),
                  _resident((3, 3 * W)), _resident((2, W)), _resident((2, 64, W)),
                  _resident((2, W)), _resident((2, 64, W)), _resident((1, W)), _resident((1, W)),
                  _resident((1, W)), _resident((W, W))],
        out_specs=[out_spec] * 10,
        out_shape=[jax.ShapeDtypeStruct((n_rows, W), F32)] * 10,
        compiler_params=_cparams(("arbitrary",)),
        name="rwkv_prep",
    )(p_rkv, p_rkv, p_rkv, p_lora, conv_w, w0, w2, a0, a2, k_k, k_a, r_k, ones_bd)


def _bd(x, bd_mask):
    xb = x.astype(BF16)
    return jnp.where(bd_mask, jnp.concatenate([xb] * SCAN_HEADS, axis=0), jnp.zeros((), BF16))


def _mm(a, b):
    return jnp.dot(a.astype(BF16), b, preferred_element_type=F32)


def _diag_blocks(f):
    lane_head = lax.broadcasted_iota(jnp.int32, (SCAN_L, 256), 1) >> 6
    out = jnp.zeros((SCAN_L, 256), F32)
    for h in range(SCAN_HEADS):
        out = jnp.where(lane_head == h, f[SCAN_L * h:SCAN_L * (h + 1)], out)
    return out


def _chunk_step(chains, states, reverse):
    L = SCAN_L
    n = len(chains)
    row = lax.broadcasted_iota(jnp.int32, (L, 256), 0)
    idx = lax.broadcasted_iota(jnp.int32, (L, 256), 1) & (L - 1)
    incl = (idx >= row) if reverse else (idx <= row)
    strict = (idx > row) if reverse else (idx < row)
    eye = (idx == row).astype(F32)
    bd_mask = ((lax.broadcasted_iota(jnp.int32, (256, 256), 0) >> 6)
               == (lax.broadcasted_iota(jnp.int32, (256, 256), 1) >> 6))
    r64 = lax.broadcasted_iota(jnp.int32, (L, L), 0)
    c64 = lax.broadcasted_iota(jnp.int32, (L, L), 1)
    tri = ((c64 >= r64) if reverse else (c64 <= r64)).astype(BF16)
    nt = (((1,), (1,)), ((), ()))
    tn = (((0,), (0,)), ((), ()))
    bd = lambda x: _bd(x, bd_mask)
    last = 0 if reverse else L - 1

    def cum_log_decay(lw):
        h1 = lw.astype(BF16)
        r1 = lw - h1.astype(F32)
        h2 = r1.astype(BF16)
        h3 = (r1 - h2.astype(F32)).astype(BF16)
        return (jnp.dot(tri, h1, preferred_element_type=F32) + jnp.dot(tri, h2, preferred_element_type=F32)
                + jnp.dot(tri, h3, preferred_element_type=F32))

    lcs = [cum_log_decay(c[3]) for c in chains]
    pre = []
    for (r, v, kk, lw, kka, km), lc in zip(chains, lcs):
        e_in = jnp.exp(lc)
        e_neg = jnp.exp(-lc)
        p_end = e_in[last:last + 1]
        bb = kka * e_neg
        kb = km * e_neg
        pre.append(dict(ab=-kk * jnp.exp(lc - lw), bb=bb, kb=kb, rb=r * e_in, bh=bb * p_end, kh=kb * p_end,
                        p_end=p_end, v=v))
    lhs = [jnp.concatenate([p["ab"], p["rb"]], axis=0).astype(BF16) for p in pre]
    g1 = [lax.dot_general(l, bd(p["bb"]), nt, preferred_element_type=F32) for l, p in zip(lhs, pre)]
    g2 = [lax.dot_general(l, bd(p["kb"]), nt, preferred_element_type=F32) for l, p in zip(lhs, pre)]
    aab = [jnp.where(strict, g[:L], 0.0) for g in g1]
    brb = [jnp.where(incl, g[L:], 0.0) for g in g1]
    aak = [jnp.where(strict, g[:L], 0.0) for g in g2]
    brk = [jnp.where(incl, g[L:], 0.0) for g in g2]

    def coupling(m):
        sh = m.bit_length() - 1
        tb, sb = row >> sh, idx >> sh
        later, earlier = (sb, tb) if reverse else (tb, sb)
        return ((row >> (sh + 1)) == (idx >> (sh + 1))) & ((later & 1) == 1) & ((earlier & 1) == 0)

    cm = coupling(1)
    tw = [eye + jnp.where(cm, a, 0.0) for a in aab]
    m = 2
    while m < L:
        cm = coupling(m)
        x = [_mm(t, bd(jnp.where(cm, a, 0.0))) for t, a in zip(tw, aab)]
        tw = [t + _mm(xi, bd(t)) for t, xi in zip(tw, x)]
        m *= 2

    v_bd = [bd(p["v"]) for p in pre]
    av = [_mm(a, vb) for a, vb in zip(aak, v_bd)]
    wu = [_mm(t, jnp.concatenate([bd(p["ab"]), bd(a)], axis=1)) for t, p, a in zip(tw, pre, av)]
    qy = [_mm(b, jnp.concatenate([bd(w[:, :256]), bd(w[:, 256:])], axis=1)) for b, w in zip(brb, wu)]
    y2b = [_mm(b, vb) for b, vb in zip(brk, v_bd)]
    f1 = [lax.dot_general(p["bh"].astype(BF16), w.astype(BF16), tn, preferred_element_type=F32)
          for p, w in zip(pre, wu)]
    f2 = [lax.dot_general(p["kh"].astype(BF16), p["v"].astype(BF16), tn, preferred_element_type=F32)
          for p in pre]
    ys, new_states = [], []
    for i in range(n):
        q = pre[i]["rb"] + qy[i][:, :256]
        y2 = qy[i][:, 256:] + y2b[i]
        mw = eye * pre[i]["p_end"] + _diag_blocks(f1[i][:, :256])
        nw = _diag_blocks(f1[i][:, 256:]) + _diag_blocks(f2[i])
        qm = jnp.concatenate([q, mw], axis=0).astype(BF16)
        z = jnp.dot(qm, bd(states[i]), preferred_element_type=F32)
        ys.append(z[:L] + y2)
        new_states.append(z[L:] + nw)
    return ys, new_states


def _chunk_scan_kernel(*refs, reverse, n_batch):
    ins = [refs[6 * b:6 * (b + 1)] for b in range(n_batch)]
    y_ref = refs[6 * n_batch]
    s_ref = refs[6 * n_batch + 1]

    @pl.when(pl.program_id(0) == 0)
    def _():
        s_ref[...] = jnp.zeros_like(s_ref)

    groups = [(b, slice(256 * g, 256 * (g + 1))) for b in range(n_batch)
              for g in range(RWKV_HEADS // SCAN_HEADS)]
    chains = [tuple(ref[:, sl] for ref in ins[b]) for b, sl in groups]
    states = [s_ref[b, :, sl] for b, sl in groups]
    ys, new_states = _chunk_step(chains, states, reverse)
    for (b, sl), y, s_new in zip(groups, ys, new_states):
        y_ref[b, :, sl] = y
        s_ref[b, :, sl] = s_new


def _rwkv_scan_dir(rows, r, v, kk, lw, kka, km, reverse):
    B, T, C = rows.B, rows.T, rows.C
    L = SCAN_L
    nctx, nlat = C // L, T // L

    def in_blk(c, b):
        j_ctx = (nctx - 1 - c) if reverse else c
        j_lat = (nlat - 1 - (c - nctx)) if reverse else (c - nctx)
        return (jnp.where(c < nctx, rows.n_lat // L + b * nctx + j_ctx, b * nlat + j_lat), 0)

    def out_blk(c):
        j = jnp.maximum(c - nctx, 0)
        return (0, (nlat - 1 - j) if reverse else j, 0)

    in_specs, args = [], []
    for b in range(B):
        in_specs += [pl.BlockSpec((L, RWKV_WIDTH), functools.partial(in_blk, b=b))] * 6
        args += [r, v, kk, lw, kka, km]
    y = pl.pallas_call(
        functools.partial(_chunk_scan_kernel, reverse=reverse, n_batch=B),
        grid=(nctx + nlat,),
        in_specs=in_specs,
        out_specs=pl.BlockSpec((B, L, RWKV_WIDTH), out_blk),
        out_shape=jax.ShapeDtypeStruct((B, T, RWKV_WIDTH), F32),
        scratch_shapes=[pltpu.VMEM((B, HEAD_DIM, RWKV_WIDTH), F32)],
        compiler_params=_cparams(("arbitrary",)),
        name="rwkv_scan_rev" if reverse else "rwkv_scan_fwd",
    )(*args)
    return y.reshape(rows.n_lat, RWKV_WIDTH)


def _cd_out_kernel(y0_ref, y1_ref, bonus_ref, lora_ref, att_ref, h_ref, mod_ref,
                   lw_ref, lb_ref, gu_ref, wo_ref, ones_ref, o_ref):
    ones_bd = ones_ref[...]
    inv = 1.0 / HEAD_DIM
    y = y0_ref[...] + y1_ref[...]
    mu = _seg_sum(y, ones_bd) * inv
    yc = y - mu
    var = _seg_sum(yc * yc, ones_bd) * inv
    yn = yc * lax.rsqrt(var + GN_EPS) * lw_ref[...] + lb_ref[...]
    gate = jnp.dot(_sigmoid(lora_ref[...][:, 128:256]).astype(BF16), gu_ref[...], preferred_element_type=F32)
    mix = ((yn + bonus_ref[...]) * gate).astype(BF16)
    out = (jnp.dot(att_ref[...], wo_ref[0:ATT_WIDTH, :], preferred_element_type=F32)
           + jnp.dot(mix, wo_ref[ATT_WIDTH:, :], preferred_element_type=F32))
    o_ref[...] = h_ref[...] + mod_ref[0][:, 2 * D:3 * D] * out


def _cd_out(rows, y0, y1, bonus, p_lora, att, h, mod3, layer, lnx_w, lnx_b, g_up, wo, ones_bd):
    W = RWKV_WIDTH
    tile = lambda n: pl.BlockSpec((rows.tm, n), lambda i: (i, 0))
    return pl.pallas_call(
        _cd_out_kernel,
        grid=(rows.lat_tiles,),
        in_specs=[tile(W)] * 3 + [tile(CD_SPLITS[2]), tile(ATT_WIDTH), tile(D),
                  pl.BlockSpec((1, 1, N_MOD * D), lambda i: (layer * 8 + rows.mod_row(i), 0, 0)),
                  _resident((1, W)), _resident((1, W)),
                  _resident((128, W)), _resident((D, D)), _resident((W, W))],
        out_specs=tile(D),
        out_shape=jax.ShapeDtypeStruct((rows.n_lat, D), F32),
        compiler_params=_cparams(("arbitrary",)),
        name="cd_out",
    )(y0, y1, bonus, p_lora, att, h, mod3, lnx_w, lnx_b, g_up, wo, ones_bd)


def _block_diag(blocks):
    n = blocks.shape[0]
    g = blocks.shape[1]
    out = jnp.zeros((n * g, n * g), blocks.dtype)
    for j in range(n):
        out = out.at[j * g:(j + 1) * g, j * g:(j + 1) * g].set(blocks[j])
    return out


def kernel(x, c, ctx, c_ctx, ada_w, ada_b, norm1, norm2, ffn_up, ffn_conv, ffn_down, ab_w_in, pool_w,
           pool_scale, fourier_w, ab_w_out, cd_w_in, q_norm, k_norm, attn_sink, rwkv_conv, rwkv_w0,
           rwkv_w2, rwkv_a0, rwkv_a2, rwkv_k_k, rwkv_k_a, rwkv_r_k, rwkv_lnx_w, rwkv_lnx_b,
           rwkv_g_up, cd_w_out):
    B, T, _ = x.shape
    C = ctx.shape[1]
    depth = ada_w.shape[0]
    assert depth == 2 and B + 1 <= 8
    rows = _Rows(B, T, C)
    wide = _Rows(B, T, C, TM_WIDE)
    W = RWKV_WIDTH

    cc = jnp.zeros((8, D), F32).at[:B].set(c).at[B].set(c_ctx)
    mod3 = _modulation(cc, ada_w, ada_b).reshape(depth * 8, 1, N_MOD * D)
    ones_bd = _block_diag(jnp.ones((RWKV_HEADS, HEAD_DIM, HEAD_DIM), BF16))

    x_lat, x_ctx = x.reshape(B * T, D), ctx.reshape(B * C, D)

    z = _in_proj(wide, x_lat, x_ctx, norm1[0].reshape(1, D), mod3, 0, ab_w_in[0].astype(BF16), "ab_in")
    spec_lat, spec_ctx = _fourier_spec(rows, z)
    h = _ab_out(rows, z, spec_lat, spec_ctx, x_lat, x_ctx, mod3, 0, _block_diag(pool_w[0]).astype(BF16),
                pool_scale[0].reshape(1, POOL_WIDTH), fourier_w[0].astype(BF16), ab_w_out[0].astype(BF16))
    h = _ffn(rows, h, norm2[0].reshape(1, D), mod3, 0, ffn_up[0].astype(BF16), ffn_conv[0],
             ffn_down[0].astype(BF16))

    qr, kr, vr, p_rkv, p_lora = _cd_in(wide, h, norm1[1].reshape(1, D), mod3, 1, cd_w_in[0].astype(BF16),
                                       q_norm[0], k_norm[0], ones_bd)
    att = _attention(rows, qr, kr, vr, attn_sink[0])
    r, v, kk, w0, kka0, km0, w1, kka1, km1, bonus = _rwkv_prep(
        rows, p_rkv, p_lora, rwkv_conv[0], rwkv_w0[0], rwkv_w2[0].astype(BF16), rwkv_a0[0],
        rwkv_a2[0].astype(BF16), rwkv_k_k[0].reshape(1, W), rwkv_k_a[0].reshape(1, W),
        rwkv_r_k[0].reshape(1, W), ones_bd)
    y0 = _rwkv_scan_dir(rows, r, v, kk, w0, kka0, km0, False)
    y1 = _rwkv_scan_dir(rows, r, v, kk, w1, kka1, km1, True)
    h_lat = _cd_out(wide, y0, y1, bonus, p_lora, att, h, mod3, 1,
                    rwkv_lnx_w[0].reshape(1, W), rwkv_lnx_b[0].reshape(1, W),
                    rwkv_g_up[0].astype(BF16), cd_w_out[0].astype(BF16), ones_bd)
    lat_rows = _Rows(B, T, C)
    h_lat = _ffn(lat_rows, h_lat, norm2[1].reshape(1, D), mod3, 1, ffn_up[1].astype(BF16), ffn_conv[1],
                 ffn_down[1].astype(BF16))
    return h_lat.reshape(B, T, D)
```

```python
import functools

import numpy as np
import jax
import jax.numpy as jnp
from jax import lax
from jax.experimental import pallas as pl
from jax.experimental.pallas import tpu as pltpu

F32 = jnp.float32
BF16 = jnp.bfloat16

D = 1024
N_MOD = 6
EPS = 1e-6
GRID_W = 64
POOL_WINDOWS = (2, 4, 8, 16)
POOL_GROUP = 192
POOL_WIDTH = 768
FOURIER_WIDTH = 256
HEAD_DIM = 64
N_Q_HEADS = 8
N_KV_HEADS = 2
GQA_GROUP = 4
ATT_BLOCK = 128
ROPE_THETA = 10000.0
RWKV_HEADS = 8
RWKV_WIDTH = 512
GN_EPS = 64e-5
ATT_WIDTH = 512
KV_WIDTH = 128
D_FF = 2816
CD_SPLITS = (768, 1536, 256)

TM = 256
TM_WIDE = 512
HALO = 8
FF_CHUNK = 256
FFT_N2 = 64
FFT_SUB = 8
SCAN_L = 64
SCAN_HEADS = 4
VMEM_LIMIT = 56 * 1024 * 1024


def _cparams(sem):
    return pltpu.CompilerParams(dimension_semantics=sem, vmem_limit_bytes=VMEM_LIMIT)


def _resident(shape):
    nd = len(shape)
    return pl.BlockSpec(shape, lambda *_: (0,) * nd, pipeline_mode=pl.Buffered(1))


def _norm_mod(x, g, shift, scale):
    ms = jnp.mean(x * x, axis=-1, keepdims=True)
    y = x * lax.rsqrt(ms + EPS) * g
    return y * (1.0 + scale) + shift


def _seg_sum(x, ones_bd):
    hi = x.astype(BF16)
    lo = (x - hi.astype(F32)).astype(BF16)
    return (jnp.dot(hi, ones_bd, preferred_element_type=F32)
            + jnp.dot(lo, ones_bd, preferred_element_type=F32))


def _sigmoid(x):
    return 1.0 / (1.0 + jnp.exp(-x))


class _Rows:
    def __init__(self, B, T, C, tm=TM):
        assert T % tm == 0 and (B * C) % tm == 0 and (B * T) % C == 0
        self.tm = tm
        assert T & (T - 1) == 0 and C & (C - 1) == 0
        self.B, self.T, self.C = B, T, C
        self.n_lat = B * T
        self.n_tot = B * T + B * C
        self.lat_tiles = self.n_lat // tm
        self.tot_tiles = self.n_tot // tm
        self.tiles_per_seq = T // tm

    def mod_row(self, i):
        return jnp.where(i < self.lat_tiles, i // self.tiles_per_seq, self.B)

    def seq_pos(self, i):
        is_lat = i < self.lat_tiles
        seqlen = jnp.where(is_lat, self.T, self.C)
        row0 = i * self.tm - jnp.where(is_lat, 0, self.n_lat)
        return lax.rem(row0, seqlen), seqlen

    def halo_specs(self, width, n_rows, col=0):
        blocks = n_rows // HALO
        per = self.tm // HALO
        prev = pl.BlockSpec((HALO, width), lambda i: (jnp.maximum(i * per - 1, 0), col))
        nxt = pl.BlockSpec((HALO, width), lambda i: (jnp.minimum((i + 1) * per, blocks - 1), col))
        return prev, nxt


def _dwconv3_ext(xe, w, pos0, seqlen):
    n = TM + 2 * HALO
    pos = pos0 + lax.broadcasted_iota(jnp.int32, (TM, 1), 0)
    xm = pltpu.roll(xe, 1, 0)[HALO:HALO + TM]
    xc = xe[HALO:HALO + TM]
    xn = pltpu.roll(xe, n - 1, 0)[HALO:HALO + TM]
    xm = jnp.where(pos == 0, 0.0, xm)
    xn = jnp.where(pos == seqlen - 1, 0.0, xn)
    return xm * w[0:1] + xc * w[1:2] + xn * w[2:3]


def _mod_kernel(cc_ref, w_ref, b_ref, o_ref):
    x = cc_ref[...]
    a = (x * _sigmoid(x)).astype(BF16)
    o_ref[0] = jnp.dot(a, w_ref[0].astype(BF16), preferred_element_type=F32) + b_ref[0]


def _modulation(cc, ada_w, ada_b):
    depth = ada_w.shape[0]
    nb = N_MOD * D // 1024
    return pl.pallas_call(
        _mod_kernel,
        grid=(depth, nb),
        in_specs=[pl.BlockSpec((8, D), lambda l, j: (0, 0)),
                  pl.BlockSpec((1, D, 1024), lambda l, j: (l, 0, j)),
                  pl.BlockSpec((1, 1, 1024), lambda l, j: (l, 0, j))],
        out_specs=pl.BlockSpec((1, 8, 1024), lambda l, j: (l, 0, j)),
        out_shape=jax.ShapeDtypeStruct((depth, 8, N_MOD * D), F32),
        compiler_params=_cparams(("arbitrary", "arbitrary")),
        name="adaln_mod",
    )(cc, ada_w, ada_b.reshape(depth, 1, N_MOD * D))


def _split_rows_specs(rows, width):
    lat = pl.BlockSpec((rows.tm, width), lambda i: (jnp.minimum(i, rows.lat_tiles - 1), 0))
    ctx = pl.BlockSpec((rows.tm, width), lambda i: (jnp.maximum(i - rows.lat_tiles, 0), 0))
    return lat, ctx


def _pick_rows(rows, lat_ref, ctx_ref):
    return jnp.where(pl.program_id(0) < rows.lat_tiles, lat_ref[...], ctx_ref[...])


def _in_kernel(x_ref, c_ref, g_ref, mod_ref, w_ref, o_ref, *, rows):
    m = mod_ref[0]
    u = _norm_mod(_pick_rows(rows, x_ref, c_ref), g_ref[...], m[:, 0:D], m[:, D:2 * D]).astype(BF16)
    o_ref[...] = jnp.dot(u, w_ref[...], preferred_element_type=F32)


def _in_proj(rows, x_lat, x_ctx, g, mod3, layer, w_bf16, name):
    width = w_bf16.shape[1]
    lat_spec, ctx_spec = _split_rows_specs(rows, D)
    return pl.pallas_call(
        functools.partial(_in_kernel, rows=rows),
        grid=(rows.tot_tiles,),
        in_specs=[lat_spec, ctx_spec,
                  _resident((1, D)),
                  pl.BlockSpec((1, 1, N_MOD * D), lambda i: (layer * 8 + rows.mod_row(i), 0, 0)),
                  _resident((D, width))],
        out_specs=pl.BlockSpec((rows.tm, width), lambda i: (i, 0)),
        out_shape=jax.ShapeDtypeStruct((rows.n_tot, width), F32),
        compiler_params=_cparams(("arbitrary",)),
        name=name,
    )(x_lat, x_ctx, g, mod3, w_bf16)


def _fft_tables(T):
    n1 = T // FFT_N2
    c = np.arange(FOURIER_WIDTH)
    ang = 2.0 * np.pi * (np.outer(c, c) % FOURIER_WIDTH) / FOURIER_WIDTH
    fc = np.concatenate([np.cos(ang), -np.sin(ang)], axis=1) / np.sqrt(FOURIER_WIDTH)
    t1 = np.arange(n1)[None, None, :]
    k1 = np.arange(n1)[None, :, None]
    t2 = np.arange(FFT_N2)[:, None, None]
    ang = 2.0 * np.pi * (((FFT_N2 * t1 + t2) * k1) % T) / T
    cm, sm = np.cos(ang) / np.sqrt(n1), np.sin(ang) / np.sqrt(n1)
    m1 = np.concatenate([np.concatenate([cm, sm], axis=2),
                         np.concatenate([-sm, cm], axis=2)], axis=1)
    k2 = np.arange(FFT_N2)
    ang = 2.0 * np.pi * (np.outer(k2, k2) % FFT_N2) / FFT_N2
    cs2 = np.concatenate([np.cos(ang), np.sin(ang)], axis=1) / np.sqrt(FFT_N2)
    return (jnp.asarray(fc, F32), jnp.asarray(m1, F32), jnp.asarray(cs2, F32))


def _ctx_fft_table(C):
    t = np.arange(C)
    ang = 2.0 * np.pi * (np.outer(t, t) % C) / C
    return jnp.asarray(np.concatenate([np.cos(ang), np.sin(ang)], axis=1) / np.sqrt(C), F32)


def _fft1_kernel(z_ref, fc_ref, m_ref, y_ref):
    fc = fc_ref[...].astype(BF16)
    for j in range(FFT_SUB):
        a = jnp.dot(z_ref[:, j, :].astype(BF16), fc, preferred_element_type=F32)
        st = jnp.concatenate([a[:, :FOURIER_WIDTH], a[:, FOURIER_WIDTH:]], axis=0).astype(BF16)
        y_ref[0, j] = jnp.dot(m_ref[j].astype(BF16), st, preferred_element_type=F32)


def _fft2_kernel(yr_ref, yi_ref, cs_ref, o_ref):
    cs = cs_ref[...].astype(BF16)
    for j in range(FFT_SUB):
        st = jnp.concatenate([yr_ref[0, :, j, :], yi_ref[0, :, j, :]], axis=0).astype(BF16)
        o_ref[0, :, j, :] = jnp.dot(cs, st, preferred_element_type=F32)


def _ctx_fft_kernel(z_ref, fc_ref, ct_ref, o_ref):
    a = jnp.dot(z_ref[...].astype(BF16), fc_ref[...].astype(BF16), preferred_element_type=F32)
    st = jnp.concatenate([a[:, :FOURIER_WIDTH], a[:, FOURIER_WIDTH:]], axis=0).astype(BF16)
    o_ref[...] = jnp.dot(ct_ref[...].astype(BF16), st, preferred_element_type=F32)


def _fourier_spec(rows, z):
    B, T, C = rows.B, rows.T, rows.C
    n1 = T // FFT_N2
    fc, m1, cs2 = _fft_tables(T)
    zv = z.reshape(rows.n_tot // FFT_N2, FFT_N2, D)
    lane_blocks = D // FOURIER_WIDTH
    ybuf = pl.pallas_call(
        _fft1_kernel,
        grid=(B, FFT_N2 // FFT_SUB),
        in_specs=[pl.BlockSpec((n1, FFT_SUB, FOURIER_WIDTH), lambda b, j: (b, j, lane_blocks - 1)),
                  _resident((FOURIER_WIDTH, 2 * FOURIER_WIDTH)),
                  pl.BlockSpec((FFT_SUB, 2 * n1, 2 * n1), lambda b, j: (j, 0, 0))],
        out_specs=pl.BlockSpec((1, FFT_SUB, 2 * n1, FOURIER_WIDTH), lambda b, j: (b, j, 0, 0)),
        out_shape=jax.ShapeDtypeStruct((B, FFT_N2, 2 * n1, FOURIER_WIDTH), F32),
        compiler_params=_cparams(("arbitrary", "arbitrary")),
        name="fft_stage1",
    )(zv, fc, m1)
    nblk = n1 // FFT_SUB
    spec_lat = pl.pallas_call(
        _fft2_kernel,
        grid=(B, nblk),
        in_specs=[pl.BlockSpec((1, FFT_N2, FFT_SUB, FOURIER_WIDTH), lambda b, j: (b, 0, j, 0)),
                  pl.BlockSpec((1, FFT_N2, FFT_SUB, FOURIER_WIDTH), lambda b, j: (b, 0, j + nblk, 0)),
                  _resident((FFT_N2, 2 * FFT_N2))],
        out_specs=pl.BlockSpec((1, FFT_N2, FFT_SUB, FOURIER_WIDTH), lambda b, j: (b, 0, j, 0)),
        out_shape=jax.ShapeDtypeStruct((B, FFT_N2, n1, FOURIER_WIDTH), F32),
        compiler_params=_cparams(("arbitrary", "arbitrary")),
        name="fft_stage2",
    )(ybuf, ybuf, cs2)
    ct = _ctx_fft_table(C)
    spec_ctx = pl.pallas_call(
        _ctx_fft_kernel,
        grid=(B,),
        in_specs=[pl.BlockSpec((C, FOURIER_WIDTH), lambda b: (rows.n_lat // C + b, lane_blocks - 1)),
                  _resident((FOURIER_WIDTH, 2 * FOURIER_WIDTH)),
                  _resident((C, 2 * C))],
        out_specs=pl.BlockSpec((C, FOURIER_WIDTH), lambda b: (b, 0)),
        out_shape=jax.ShapeDtypeStruct((B * C, FOURIER_WIDTH), F32),
        compiler_params=_cparams(("arbitrary",)),
        name="fft_ctx",
    )(z, fc, ct)
    return spec_lat.reshape(rows.n_lat, FOURIER_WIDTH), spec_ctx


def _ab_out_kernel(z_ref, zp_ref, zn_ref, spec_ref, spec_c_ref, h_ref, h_c_ref, mod_ref, pw_ref, ps_ref,
                   fw_ref, wo_ref, o_ref, *, rows):
    i = pl.program_id(0)
    pos0, seqlen = rows.seq_pos(i)
    n = TM + 2 * HALO
    ze = jnp.concatenate([zp_ref[...], z_ref[...], zn_ref[...]], axis=0)
    pe = pos0 - HALO + lax.broadcasted_iota(jnp.int32, (n, 1), 0)
    ze = jnp.where((pe >= 0) & (pe < seqlen), ze, 0.0)
    sums = {1: ze}
    w = 1
    while w < max(POOL_WINDOWS):
        sums[2 * w] = sums[w] + pltpu.roll(sums[w], n - w, 0)
        w *= 2
    pos = pos0 + lax.broadcasted_iota(jnp.int32, (TM, 1), 0)
    lane = lax.broadcasted_iota(jnp.int32, (1, POOL_WIDTH), 1)
    zc = ze[HALO:HALO + TM]
    mean = jnp.zeros((TM, POOL_WIDTH), F32)
    for g, w in enumerate(POOL_WINDOWS):
        left = w // 2
        right = w - 1 - left
        cw = pltpu.roll(sums[w], left, 0)[HALO:HALO + TM]
        cnt = (jnp.minimum(pos + right + 1, seqlen) - jnp.maximum(pos - left, 0)).astype(F32)
        in_group = (lane >= g * POOL_GROUP) & (lane < (g + 1) * POOL_GROUP)
        mean = jnp.where(in_group, cw / cnt, mean)
    pooled = (mean - zc).astype(BF16)
    ya = jnp.dot(pooled, pw_ref[...], preferred_element_type=F32) * ps_ref[...]
    yb = jnp.dot(_pick_rows(rows, spec_ref, spec_c_ref).astype(BF16), fw_ref[...], preferred_element_type=F32)
    y = (jnp.dot(ya.astype(BF16), wo_ref[0:POOL_WIDTH, :], preferred_element_type=F32)
         + jnp.dot(yb.astype(BF16), wo_ref[POOL_WIDTH:, :], preferred_element_type=F32))
    gate = mod_ref[0][:, 2 * D:3 * D]
    o_ref[...] = _pick_rows(rows, h_ref, h_c_ref) + gate * y


def _ab_out(rows, z, spec_lat, spec_ctx, x_lat, x_ctx, mod3, layer, pw_bd, pscale, fw, wo):
    n_rows = rows.n_tot
    zprev, znext = rows.halo_specs(POOL_WIDTH, n_rows)
    return pl.pallas_call(
        functools.partial(_ab_out_kernel, rows=rows),
        grid=(n_rows // TM,),
        in_specs=[pl.BlockSpec((TM, POOL_WIDTH), lambda i: (i, 0)), zprev, znext,
                  *_split_rows_specs(rows, FOURIER_WIDTH), *_split_rows_specs(rows, D),
                  pl.BlockSpec((1, 1, N_MOD * D), lambda i: (layer * 8 + rows.mod_row(i), 0, 0)),
                  _resident((POOL_WIDTH, POOL_WIDTH)), _resident((1, POOL_WIDTH)),
                  _resident((FOURIER_WIDTH, FOURIER_WIDTH)), _resident((D, D))],
        out_specs=pl.BlockSpec((TM, D), lambda i: (i, 0)),
        out_shape=jax.ShapeDtypeStruct((n_rows, D), F32),
        compiler_params=_cparams(("arbitrary",)),
        name="ab_out",
    )(z, z, z, spec_lat, spec_ctx, x_lat, x_ctx, mod3, pw_bd, pscale, fw, wo)


def _ffn_kernel(h_ref, hp_ref, hn_ref, g_ref, mod_ref, wu_ref, cw_ref, wd_ref, o_ref, up_ref, act_ref, *, rows):
    i = pl.program_id(0)
    pos0, seqlen = rows.seq_pos(i)
    tm = rows.tm
    n = tm + 2 * HALO
    m = mod_ref[0]
    he = jnp.concatenate([hp_ref[...], h_ref[...], hn_ref[...]], axis=0)
    u = _norm_mod(he, g_ref[...], m[:, 3 * D:4 * D], m[:, 4 * D:5 * D])
    pe = pos0 - HALO + lax.broadcasted_iota(jnp.int32, (n, 1), 0)
    u = jnp.where((pe >= 0) & (pe < seqlen), u, 0.0).astype(BF16)

    def conv(slot, col):
        w = cw_ref[:, col:col + FF_CHUNK]
        return (up_ref[slot, HALO - 1:HALO - 1 + tm, :] * w[0:1] + up_ref[slot, HALO:HALO + tm, :] * w[1:2]
                + up_ref[slot, HALO + 1:HALO + 1 + tm, :] * w[2:3])

    for c in range(D_FF // FF_CHUNK):
        lo = c * FF_CHUNK
        sg, sv = 2 * (c % 2), 2 * (c % 2) + 1
        up_ref[sg] = jnp.dot(u, wu_ref[:, lo:lo + FF_CHUNK], preferred_element_type=F32)
        up_ref[sv] = jnp.dot(u, wu_ref[:, D_FF + lo:D_FF + lo + FF_CHUNK], preferred_element_type=F32)
        gate = conv(sg, lo)
        val = conv(sv, D_FF + lo)
        half = 0.5 * gate
        act_ref[:, lo:lo + FF_CHUNK] = ((half + half * jnp.tanh(half)) * val).astype(BF16)
    acc = jnp.dot(act_ref[...], wd_ref[...], preferred_element_type=F32)
    o_ref[...] = h_ref[...] + m[:, 5 * D:6 * D] * acc


def _ffn(rows, h, g, mod3, layer, wu, cw, wd):
    n_rows = h.shape[0]
    tm = rows.tm
    hprev, hnext = rows.halo_specs(D, n_rows)
    return pl.pallas_call(
        functools.partial(_ffn_kernel, rows=rows),
        grid=(n_rows // tm,),
        in_specs=[pl.BlockSpec((tm, D), lambda i: (i, 0)), hprev, hnext,
                  _resident((1, D)),
                  pl.BlockSpec((1, 1, N_MOD * D), lambda i: (layer * 8 + rows.mod_row(i), 0, 0)),
                  _resident((D, 2 * D_FF)), _resident((3, 2 * D_FF)), _resident((D_FF, D))],
        out_specs=pl.BlockSpec((tm, D), lambda i: (i, 0)),
        out_shape=jax.ShapeDtypeStruct((n_rows, D), F32),
        scratch_shapes=[pltpu.VMEM((4, tm + 2 * HALO, FF_CHUNK), F32), pltpu.VMEM((tm, D_FF), BF16)],
        compiler_params=_cparams(("arbitrary",)),
        name="conv_ffn",
    )(h, h, h, g, mod3, wu, cw, wd)


def _rope_tables(T, tm):
    half = HEAD_DIM // 2
    inv = ROPE_THETA ** (-np.arange(0, half, 2, dtype=np.float64) / half)
    t = np.arange(T)
    ang_r = (t // GRID_W)[:, None] * inv[None, :]
    ang_c = (t % GRID_W)[:, None] * inv[None, :]
    cos = np.concatenate([np.cos(ang_r), np.cos(ang_r), np.cos(ang_c), np.cos(ang_c)], axis=1)
    sin = np.concatenate([-np.sin(ang_r), np.sin(ang_r), -np.sin(ang_c), np.sin(ang_c)], axis=1)
    cos = np.concatenate([np.tile(cos, (1, 2)), np.ones((tm, 128))], axis=0)
    sin = np.concatenate([np.tile(sin, (1, 2)), np.zeros((tm, 128))], axis=0)
    return jnp.asarray(cos, F32), jnp.asarray(sin, F32)


def _rope128(x, cos, sin):
    lane = lax.broadcasted_iota(jnp.int32, (1, 128), 1)
    partner = jnp.where((lane & 31) < 16, pltpu.roll(x, 128 - 16, 1), pltpu.roll(x, 16, 1))
    return x * cos + partner * sin


def _cd_in_kernel(x_ref, g_ref, mod_ref, w_ref, cos_ref, sin_ref, qn_ref, kn_ref, ones_ref,
                  q_out, k_out, v_out, rkv_out, lora_out):
    m = mod_ref[0]
    u = _norm_mod(x_ref[...], g_ref[...], m[:, 0:D], m[:, D:2 * D]).astype(BF16)
    n_att, n_rkv, n_lora = CD_SPLITS
    rkv_out[...] = jnp.dot(u, w_ref[:, n_att:n_att + n_rkv], preferred_element_type=F32)
    lora_out[...] = jnp.dot(u, w_ref[:, n_att + n_rkv:], preferred_element_type=F32)
    p = jnp.dot(u, w_ref[:, 0:n_att], preferred_element_type=F32)
    cos, sin = cos_ref[...], sin_ref[...]
    q = p[:, 0:ATT_WIDTH]
    ms = _seg_sum(q * q, ones_ref[...]) * (1.0 / HEAD_DIM)
    q = q * lax.rsqrt(ms + EPS) * qn_ref[...]
    q = jnp.concatenate([_rope128(q[:, 128 * j:128 * (j + 1)], cos, sin) for j in range(4)], axis=1)
    q_out[...] = (q * HEAD_DIM ** -0.5).astype(BF16)
    k = p[:, ATT_WIDTH:ATT_WIDTH + KV_WIDTH]
    ms = _seg_sum(k * k, ones_ref[0:KV_WIDTH, 0:KV_WIDTH]) * (1.0 / HEAD_DIM)
    k = k * lax.rsqrt(ms + EPS) * kn_ref[...]
    k_out[...] = _rope128(k, cos, sin).astype(BF16)
    v_out[...] = p[:, ATT_WIDTH + KV_WIDTH:].astype(BF16)


def _cd_in(rows, h, g, mod3, layer, w_bf16, q_norm, k_norm, ones_bd):
    cos, sin = _rope_tables(rows.T, rows.tm)
    tps = rows.tiles_per_seq
    tab = lambda i: (jnp.where(i < rows.lat_tiles, lax.rem(i, tps), tps), 0)
    qn = jnp.tile(q_norm, N_Q_HEADS).reshape(1, ATT_WIDTH)
    kn = jnp.tile(k_norm, N_KV_HEADS).reshape(1, KV_WIDTH)
    widths = (ATT_WIDTH, KV_WIDTH, KV_WIDTH, CD_SPLITS[1], CD_SPLITS[2])
    dtypes = (BF16, BF16, BF16, F32, F32)
    return pl.pallas_call(
        _cd_in_kernel,
        grid=(rows.tot_tiles,),
        in_specs=[pl.BlockSpec((rows.tm, D), lambda i: (i, 0)),
                  _resident((1, D)),
                  pl.BlockSpec((1, 1, N_MOD * D), lambda i: (layer * 8 + rows.mod_row(i), 0, 0)),
                  _resident((D, sum(CD_SPLITS))),
                  pl.BlockSpec((rows.tm, 128), tab), pl.BlockSpec((rows.tm, 128), tab),
                  _resident((1, ATT_WIDTH)), _resident((1, KV_WIDTH)),
                  _resident((RWKV_WIDTH, RWKV_WIDTH))],
        out_specs=[pl.BlockSpec((rows.tm, n), lambda i: (i, 0)) for n in widths],
        out_shape=[jax.ShapeDtypeStruct((rows.n_tot, n), dt) for n, dt in zip(widths, dtypes)],
        compiler_params=_cparams(("arbitrary",)),
        name="cd_in",
    )(h, g, mod3, w_bf16, cos, sin, qn, kn, ones_bd)


def _attn_bias(C):
    Q = ATT_BLOCK
    ql = (np.arange(GQA_GROUP * Q) % Q)[:, None]
    col = np.arange(3 * Q + C)[None, :]
    rel = col - ql
    band = (rel >= 0) & (rel <= 2 * Q) & (col < 3 * Q)
    is_ctx = np.broadcast_to(col >= 3 * Q, band.shape)
    variants = [band & (col >= Q) | is_ctx, band | is_ctx, band & (col < 2 * Q) | is_ctx]
    return jnp.asarray(np.where(np.stack(variants), 0.0, -1e30), F32)


def _attn_kernel(q_ref, kp_ref, kc_ref, kn_ref, kx_ref, vp_ref, vc_ref, vn_ref, vx_ref, sink_ref, bias_ref,
                 o_ref, *, nq):
    i = pl.program_id(1)
    Q = ATT_BLOCK
    heads = range(N_KV_HEADS)
    k_all = jnp.concatenate([kp_ref[...], kc_ref[...], kn_ref[...], kx_ref[...]], axis=0)
    v_all = jnp.concatenate([vp_ref[...], vc_ref[...], vn_ref[...], vx_ref[...]], axis=0)
    bias = bias_ref[jnp.where(i == 0, 0, jnp.where(i == nq - 1, 2, 1))]
    q = q_ref[...]
    q4 = [jnp.concatenate([q[:, HEAD_DIM * (GQA_GROUP * hk + g):HEAD_DIM * (GQA_GROUP * hk + g + 1)]
                           for g in range(GQA_GROUP)], axis=0) for hk in heads]
    kh = [k_all[:, HEAD_DIM * hk:HEAD_DIM * (hk + 1)] for hk in heads]
    vh = [v_all[:, HEAD_DIM * hk:HEAD_DIM * (hk + 1)] for hk in heads]
    s = [lax.dot_general(q4[hk], kh[hk], (((1,), (1,)), ((), ())), preferred_element_type=F32) + bias
         for hk in heads]
    sk = [sink_ref[hk][:, 0:1] for hk in heads]
    m = [jnp.maximum(jnp.max(s[hk], axis=1, keepdims=True), sk[hk]) for hk in heads]
    p = [jnp.exp(s[hk] - m[hk]) for hk in heads]
    den = [jnp.sum(p[hk], axis=1, keepdims=True) + jnp.exp(sk[hk] - m[hk]) for hk in heads]
    o = [jnp.dot(p[hk].astype(BF16), vh[hk], preferred_element_type=F32) * (1.0 / den[hk]) for hk in heads]
    o_ref[...] = jnp.concatenate([o[hk][Q * g:Q * (g + 1)] for hk in heads for g in range(GQA_GROUP)],
                                 axis=1).astype(BF16)


def _attention(rows, qr, kr, vr, sink):
    B, T, C = rows.B, rows.T, rows.C
    nq = T // ATT_BLOCK
    assert nq >= 2
    blk = lambda off: pl.BlockSpec(
        (ATT_BLOCK, KV_WIDTH), lambda b, i: (b * nq + jnp.clip(i + off, 0, nq - 1), 0))
    ctx = pl.BlockSpec((C, KV_WIDTH), lambda b, i: (rows.n_lat // C + b, 0))
    sink_col = jnp.repeat(sink.reshape(N_KV_HEADS, GQA_GROUP), ATT_BLOCK, axis=1)
    sink_col = jnp.broadcast_to(sink_col[:, :, None], (N_KV_HEADS, GQA_GROUP * ATT_BLOCK, 128))
    return pl.pallas_call(
        functools.partial(_attn_kernel, nq=nq),
        grid=(B, nq),
        in_specs=[pl.BlockSpec((ATT_BLOCK, ATT_WIDTH), lambda b, i: (b * nq + i, 0)),
                  blk(-1), blk(0), blk(1), ctx, blk(-1), blk(0), blk(1), ctx,
                  _resident((N_KV_HEADS, GQA_GROUP * ATT_BLOCK, 128)),
                  _resident((3, GQA_GROUP * ATT_BLOCK, 3 * ATT_BLOCK + C))],
        out_specs=pl.BlockSpec((ATT_BLOCK, ATT_WIDTH), lambda b, i: (b * nq + i, 0)),
        out_shape=jax.ShapeDtypeStruct((rows.n_lat, ATT_WIDTH), BF16),
        compiler_params=_cparams(("arbitrary", "arbitrary")),
        name="window_attention",
    )(qr, kr, kr, kr, kr, vr, vr, vr, vr, sink_col, _attn_bias(C))


def _rwkv_prep_kernel(x_ref, xp_ref, xn_ref, lora_ref, cw_ref, w0_ref, w2_ref, a0_ref, a2_ref,
                      kk_ref, ka_ref, rk_ref, ones_ref,
                      r_out, v_out, kkn_out, w_out0, kka_out0, km_out0, w_out1, kka_out1, km_out1, bonus_out,
                      *, rows):
    i = pl.program_id(0)
    pos0, seqlen = rows.seq_pos(i)
    xe = jnp.concatenate([xp_ref[...], x_ref[...], xn_ref[...]], axis=0)
    rkv = _dwconv3_ext(xe, cw_ref[...], pos0, seqlen)
    W = RWKV_WIDTH
    r, k, v = rkv[:, 0:W], rkv[:, W:2 * W], rkv[:, 2 * W:3 * W]
    kk = k * kk_ref[...]
    kk = kk * lax.rsqrt(_seg_sum(kk * kk, ones_ref[...]) + EPS)
    r_out[...] = r
    v_out[...] = v
    kkn_out[...] = kk
    lora = lora_ref[...]
    wl = jnp.tanh(lora[:, 0:64]).astype(BF16)
    al = lora[:, 64:128].astype(BF16)
    outs = ((w_out0, kka_out0, km_out0), (w_out1, kka_out1, km_out1))
    bonus = jnp.zeros_like(r)
    for d in range(2):
        x = -(w0_ref[d:d + 1, :] + jnp.dot(wl, w2_ref[d], preferred_element_type=F32))
        softplus = jnp.maximum(x, 0.0) + jnp.log(1.0 + jnp.exp(-jnp.abs(x)))
        w_log = -softplus - 0.5
        a = _sigmoid(a0_ref[d:d + 1, :] + jnp.dot(al, a2_ref[d], preferred_element_type=F32))
        w_o, kka_o, km_o = outs[d]
        w_o[...] = -jnp.exp(w_log)
        kka_o[...] = kk * a
        km = k * (1.0 + (a - 1.0) * ka_ref[...])
        km_o[...] = km
        bonus = bonus + km
    bonus_out[...] = _seg_sum(r * bonus * rk_ref[...], ones_ref[...]) * v


def _rwkv_prep(rows, p_rkv, p_lora, conv_w, w0, w2, a0, a2, k_k, k_a, r_k, ones_bd):
    n_rows = rows.n_tot
    W = RWKV_WIDTH
    xprev, xnext = rows.halo_specs(3 * W, n_rows)
    out_spec = pl.BlockSpec((TM, W), lambda i: (i, 0))
    return pl.pallas_call(
        functools.partial(_rwkv_prep_kernel, rows=rows),
        grid=(rows.tot_tiles,),
        in_specs=[pl.BlockSpec((TM, 3 * W), lambda i: (i, 0)), xprev, xnext,
                  pl.BlockSpec((TM, CD_SPLITS[2]), lambda i: (i, 0)),
                  _resident((3, 3 * W)), _resident((2, W)), _resident((2, 64, W)),
                  _resident((2, W)), _resident((2, 64, W)), _resident((1, W)), _resident((1, W)),
                  _resident((1, W)), _resident((W, W))],
        out_specs=[out_spec] * 10,
        out_shape=[jax.ShapeDtypeStruct((n_rows, W), F32)] * 10,
        compiler_params=_cparams(("arbitrary",)),
        name="rwkv_prep",
    )(p_rkv, p_rkv, p_rkv, p_lora, conv_w, w0, w2, a0, a2, k_k, k_a, r_k, ones_bd)


def _bd(x, bd_mask):
    xb = x.astype(BF16)
    return jnp.where(bd_mask, jnp.concatenate([xb] * SCAN_HEADS, axis=0), jnp.zeros((), BF16))


def _mm(a, b):
    return jnp.dot(a.astype(BF16), b, preferred_element_type=F32)


def _diag_blocks(f):
    lane_head = lax.broadcasted_iota(jnp.int32, (SCAN_L, 256), 1) >> 6
    out = jnp.zeros((SCAN_L, 256), F32)
    for h in range(SCAN_HEADS):
        out = jnp.where(lane_head == h, f[SCAN_L * h:SCAN_L * (h + 1)], out)
    return out


def _chunk_step(chains, states, reverse):
    L = SCAN_L
    n = len(chains)
    row = lax.broadcasted_iota(jnp.int32, (L, 256), 0)
    idx = lax.broadcasted_iota(jnp.int32, (L, 256), 1) & (L - 1)
    incl = (idx >= row) if reverse else (idx <= row)
    strict = (idx > row) if reverse else (idx < row)
    eye = (idx == row).astype(F32)
    bd_mask = ((lax.broadcasted_iota(jnp.int32, (256, 256), 0) >> 6)
               == (lax.broadcasted_iota(jnp.int32, (256, 256), 1) >> 6))
    r64 = lax.broadcasted_iota(jnp.int32, (L, L), 0)
    c64 = lax.broadcasted_iota(jnp.int32, (L, L), 1)
    tri = ((c64 >= r64) if reverse else (c64 <= r64)).astype(BF16)
    nt = (((1,), (1,)), ((), ()))
    tn = (((0,), (0,)), ((), ()))
    bd = lambda x: _bd(x, bd_mask)
    last = 0 if reverse else L - 1

    def cum_log_decay(lw):
        h1 = lw.astype(BF16)
        r1 = lw - h1.astype(F32)
        h2 = r1.astype(BF16)
        h3 = (r1 - h2.astype(F32)).astype(BF16)
        return (jnp.dot(tri, h1, preferred_element_type=F32) + jnp.dot(tri, h2, preferred_element_type=F32)
                + jnp.dot(tri, h3, preferred_element_type=F32))

    lcs = [cum_log_decay(c[3]) for c in chains]
    pre = []
    for (r, v, kk, lw, kka, km), lc in zip(chains, lcs):
        e_in = jnp.exp(lc)
        e_neg = jnp.exp(-lc)
        p_end = e_in[last:last + 1]
        bb = kka * e_neg
        kb = km * e_neg
        pre.append(dict(ab=-kk * jnp.exp(lc - lw), bb=bb, kb=kb, rb=r * e_in, bh=bb * p_end, kh=kb * p_end,
                        p_end=p_end, v=v))
    lhs = [jnp.concatenate([p["ab"], p["rb"]], axis=0).astype(BF16) for p in pre]
    g1 = [lax.dot_general(l, bd(p["bb"]), nt, preferred_element_type=F32) for l, p in zip(lhs, pre)]
    g2 = [lax.dot_general(l, bd(p["kb"]), nt, preferred_element_type=F32) for l, p in zip(lhs, pre)]
    aab = [jnp.where(strict, g[:L], 0.0) for g in g1]
    brb = [jnp.where(incl, g[L:], 0.0) for g in g1]
    aak = [jnp.where(strict, g[:L], 0.0) for g in g2]
    brk = [jnp.where(incl, g[L:], 0.0) for g in g2]

    def coupling(m):
        sh = m.bit_length() - 1
        tb, sb = row >> sh, idx >> sh
        later, earlier = (sb, tb) if reverse else (tb, sb)
        return ((row >> (sh + 1)) == (idx >> (sh + 1))) & ((later & 1) == 1) & ((earlier & 1) == 0)

    cm = coupling(1)
    tw = [eye + jnp.where(cm, a, 0.0) for a in aab]
    m = 2
    while m < L:
        cm = coupling(m)
        x = [_mm(t, bd(jnp.where(cm, a, 0.0))) for t, a in zip(tw, aab)]
        tw = [t + _mm(xi, bd(t)) for t, xi in zip(tw, x)]
        m *= 2

    v_bd = [bd(p["v"]) for p in pre]
    av = [_mm(a, vb) for a, vb in zip(aak, v_bd)]
    wu = [_mm(t, jnp.concatenate([bd(p["ab"]), bd(a)], axis=1)) for t, p, a in zip(tw, pre, av)]
    qy = [_mm(b, jnp.concatenate([bd(w[:, :256]), bd(w[:, 256:])], axis=1)) for b, w in zip(brb, wu)]
    y2b = [_mm(b, vb) for b, vb in zip(brk, v_bd)]
    f1 = [lax.dot_general(p["bh"].astype(BF16), w.astype(BF16), tn, preferred_element_type=F32)
          for p, w in zip(pre, wu)]
    f2 = [lax.dot_general(p["kh"].astype(BF16), p["v"].astype(BF16), tn, preferred_element_type=F32)
          for p in pre]
    ys, new_states = [], []
    for i in range(n):
        q = pre[i]["rb"] + qy[i][:, :256]
        y2 = qy[i][:, 256:] + y2b[i]
        mw = eye * pre[i]["p_end"] + _diag_blocks(f1[i][:, :256])
        nw = _diag_blocks(f1[i][:, 256:]) + _diag_blocks(f2[i])
        qm = jnp.concatenate([q, mw], axis=0).astype(BF16)
        z = jnp.dot(qm, bd(states[i]), preferred_element_type=F32)
        ys.append(z[:L] + y2)
        new_states.append(z[L:] + nw)
    return ys, new_states


def _chunk_scan_kernel(*refs, reverse, n_batch):
    ins = [refs[6 * b:6 * (b + 1)] for b in range(n_batch)]
    y_ref = refs[6 * n_batch]
    s_ref = refs[6 * n_batch + 1]

    @pl.when(pl.program_id(0) == 0)
    def _():
        s_ref[...] = jnp.zeros_like(s_ref)

    groups = [(b, slice(256 * g, 256 * (g + 1))) for b in range(n_batch)
              for g in range(RWKV_HEADS // SCAN_HEADS)]
    chains = [tuple(ref[:, sl] for ref in ins[b]) for b, sl in groups]
    states = [s_ref[b, :, sl] for b, sl in groups]
    ys, new_states = _chunk_step(chains, states, reverse)
    for (b, sl), y, s_new in zip(groups, ys, new_states):
        y_ref[b, :, sl] = y
        s_ref[b, :, sl] = s_new


def _rwkv_scan_dir(rows, r, v, kk, lw, kka, km, reverse):
    B, T, C = rows.B, rows.T, rows.C
    L = SCAN_L
    nctx, nlat = C // L, T // L

    def in_blk(c, b):
        j_ctx = (nctx - 1 - c) if reverse else c
        j_lat = (nlat - 1 - (c - nctx)) if reverse else (c - nctx)
        return (jnp.where(c < nctx, rows.n_lat // L + b * nctx + j_ctx, b * nlat + j_lat), 0)

    def out_blk(c):
        j = jnp.maximum(c - nctx, 0)
        return (0, (nlat - 1 - j) if reverse else j, 0)

    in_specs, args = [], []
    for b in range(B):
        in_specs += [pl.BlockSpec((L, RWKV_WIDTH), functools.partial(in_blk, b=b))] * 6
        args += [r, v, kk, lw, kka, km]
    y = pl.pallas_call(
        functools.partial(_chunk_scan_kernel, reverse=reverse, n_batch=B),
        grid=(nctx + nlat,),
        in_specs=in_specs,
        out_specs=pl.BlockSpec((B, L, RWKV_WIDTH), out_blk),
        out_shape=jax.ShapeDtypeStruct((B, T, RWKV_WIDTH), F32),
        scratch_shapes=[pltpu.VMEM((B, HEAD_DIM, RWKV_WIDTH), F32)],
        compiler_params=_cparams(("arbitrary",)),
        name="rwkv_scan_rev" if reverse else "rwkv_scan_fwd",
    )(*args)
    return y.reshape(rows.n_lat, RWKV_WIDTH)


def _cd_out_kernel(y0_ref, y1_ref, bonus_ref, lora_ref, att_ref, h_ref, mod_ref,
                   lw_ref, lb_ref, gu_ref, wo_ref, ones_ref, o_ref):
    ones_bd = ones_ref[...]
    inv = 1.0 / HEAD_DIM
    y = y0_ref[...] + y1_ref[...]
    mu = _seg_sum(y, ones_bd) * inv
    yc = y - mu
    var = _seg_sum(yc * yc, ones_bd) * inv
    yn = yc * lax.rsqrt(var + GN_EPS) * lw_ref[...] + lb_ref[...]
    gate = jnp.dot(_sigmoid(lora_ref[...][:, 128:256]).astype(BF16), gu_ref[...], preferred_element_type=F32)
    mix = ((yn + bonus_ref[...]) * gate).astype(BF16)
    out = (jnp.dot(att_ref[...], wo_ref[0:ATT_WIDTH, :], preferred_element_type=F32)
           + jnp.dot(mix, wo_ref[ATT_WIDTH:, :], preferred_element_type=F32))
    o_ref[...] = h_ref[...] + mod_ref[0][:, 2 * D:3 * D] * out


def _cd_out(rows, y0, y1, bonus, p_lora, att, h, mod3, layer, lnx_w, lnx_b, g_up, wo, ones_bd):
    W = RWKV_WIDTH
    tile = lambda n: pl.BlockSpec((rows.tm, n), lambda i: (i, 0))
    return pl.pallas_call(
        _cd_out_kernel,
        grid=(rows.lat_tiles,),
        in_specs=[tile(W)] * 3 + [tile(CD_SPLITS[2]), tile(ATT_WIDTH), tile(D),
                  pl.BlockSpec((1, 1, N_MOD * D), lambda i: (layer * 8 + rows.mod_row(i), 0, 0)),
                  _resident((1, W)), _resident((1, W)),
                  _resident((128, W)), _resident((D, D)), _resident((W, W))],
        out_specs=tile(D),
        out_shape=jax.ShapeDtypeStruct((rows.n_lat, D), F32),
        compiler_params=_cparams(("arbitrary",)),
        name="cd_out",
    )(y0, y1, bonus, p_lora, att, h, mod3, lnx_w, lnx_b, g_up, wo, ones_bd)


def _block_diag(blocks):
    n = blocks.shape[0]
    g = blocks.shape[1]
    out = jnp.zeros((n * g, n * g), blocks.dtype)
    for j in range(n):
        out = out.at[j * g:(j + 1) * g, j * g:(j + 1) * g].set(blocks[j])
    return out


def kernel(x, c, ctx, c_ctx, ada_w, ada_b, norm1, norm2, ffn_up, ffn_conv, ffn_down, ab_w_in, pool_w,
           pool_scale, fourier_w, ab_w_out, cd_w_in, q_norm, k_norm, attn_sink, rwkv_conv, rwkv_w0,
           rwkv_w2, rwkv_a0, rwkv_a2, rwkv_k_k, rwkv_k_a, rwkv_r_k, rwkv_lnx_w, rwkv_lnx_b,
           rwkv_g_up, cd_w_out):
    B, T, _ = x.shape
    C = ctx.shape[1]
    depth = ada_w.shape[0]
    assert depth == 2 and B + 1 <= 8
    rows = _Rows(B, T, C)
    wide = _Rows(B, T, C, TM_WIDE)
    W = RWKV_WIDTH

    cc = jnp.zeros((8, D), F32).at[:B].set(c).at[B].set(c_ctx)
    mod3 = _modulation(cc, ada_w, ada_b).reshape(depth * 8, 1, N_MOD * D)
    ones_bd = _block_diag(jnp.ones((RWKV_HEADS, HEAD_DIM, HEAD_DIM), BF16))

    x_lat, x_ctx = x.reshape(B * T, D), ctx.reshape(B * C, D)

    z = _in_proj(wide, x_lat, x_ctx, norm1[0].reshape(1, D), mod3, 0, ab_w_in[0].astype(BF16), "ab_in")
    spec_lat, spec_ctx = _fourier_spec(rows, z)
    h = _ab_out(rows, z, spec_lat, spec_ctx, x_lat, x_ctx, mod3, 0, _block_diag(pool_w[0]).astype(BF16),
                pool_scale[0].reshape(1, POOL_WIDTH), fourier_w[0].astype(BF16), ab_w_out[0].astype(BF16))
    h = _ffn(rows, h, norm2[0].reshape(1, D), mod3, 0, ffn_up[0].astype(BF16), ffn_conv[0],
             ffn_down[0].astype(BF16))

    qr, kr, vr, p_rkv, p_lora = _cd_in(wide, h, norm1[1].reshape(1, D), mod3, 1, cd_w_in[0].astype(BF16),
                                       q_norm[0], k_norm[0], ones_bd)
    att = _attention(rows, qr, kr, vr, attn_sink[0])
    r, v, kk, w0, kka0, km0, w1, kka1, km1, bonus = _rwkv_prep(
        rows, p_rkv, p_lora, rwkv_conv[0], rwkv_w0[0], rwkv_w2[0].astype(BF16), rwkv_a0[0],
        rwkv_a2[0].astype(BF16), rwkv_k_k[0].reshape(1, W), rwkv_k_a[0].reshape(1, W),
        rwkv_r_k[0].reshape(1, W), ones_bd)
    y0 = _rwkv_scan_dir(rows, r, v, kk, w0, kka0, km0, False)
    y1 = _rwkv_scan_dir(rows, r, v, kk, w1, kka1, km1, True)
    h_lat = _cd_out(wide, y0, y1, bonus, p_lora, att, h, mod3, 1,
                    rwkv_lnx_w[0].reshape(1, W), rwkv_lnx_b[0].reshape(1, W),
                    rwkv_g_up[0].astype(BF16), cd_w_out[0].astype(BF16), ones_bd)
    lat_rows = _Rows(B, T, C, TM_WIDE)
    h_lat = _ffn(lat_rows, h_lat, norm2[1].reshape(1, D), mod3, 1, ffn_up[1].astype(BF16), ffn_conv[1],
                 ffn_down[1].astype(BF16))
    return h_lat.reshape(B, T, D)
```

```python
import functools

import numpy as np
import jax
import jax.numpy as jnp
from jax import lax
from jax.experimental import pallas as pl
from jax.experimental.pallas import tpu as pltpu

F32 = jnp.float32
BF16 = jnp.bfloat16

D = 1024
N_MOD = 6
EPS = 1e-6
GRID_W = 64
POOL_WINDOWS = (2, 4, 8, 16)
POOL_GROUP = 192
POOL_WIDTH = 768
FOURIER_WIDTH = 256
HEAD_DIM = 64
N_Q_HEADS = 8
N_KV_HEADS = 2
GQA_GROUP = 4
ATT_BLOCK = 128
ROPE_THETA = 10000.0
RWKV_HEADS = 8
RWKV_WIDTH = 512
GN_EPS = 64e-5
ATT_WIDTH = 512
KV_WIDTH = 128
D_FF = 2816
CD_SPLITS = (768, 1536, 256)

TM = 256
TM_WIDE = 512
HALO = 8
FF_CHUNK = 256
FFT_N2 = 64
FFT_SUB = 8
SCAN_L = 64
SCAN_HEADS = 4
VMEM_LIMIT = 56 * 1024 * 1024


def _cparams(sem):
    return pltpu.CompilerParams(dimension_semantics=sem, vmem_limit_bytes=VMEM_LIMIT)


def _resident(shape):
    nd = len(shape)
    return pl.BlockSpec(shape, lambda *_: (0,) * nd, pipeline_mode=pl.Buffered(1))


def _norm_mod(x, g, shift, scale):
    ms = jnp.mean(x * x, axis=-1, keepdims=True)
    y = x * lax.rsqrt(ms + EPS) * g
    return y * (1.0 + scale) + shift


def _seg_sum(x, ones_bd):
    hi = x.astype(BF16)
    lo = (x - hi.astype(F32)).astype(BF16)
    return (jnp.dot(hi, ones_bd, preferred_element_type=F32)
            + jnp.dot(lo, ones_bd, preferred_element_type=F32))


def _sigmoid(x):
    return 1.0 / (1.0 + jnp.exp(-x))


class _Rows:
    def __init__(self, B, T, C, tm=TM):
        assert T % tm == 0 and (B * C) % tm == 0 and (B * T) % C == 0
        self.tm = tm
        assert T & (T - 1) == 0 and C & (C - 1) == 0
        self.B, self.T, self.C = B, T, C
        self.n_lat = B * T
        self.n_tot = B * T + B * C
        self.lat_tiles = self.n_lat // tm
        self.tot_tiles = self.n_tot // tm
        self.tiles_per_seq = T // tm

    def mod_row(self, i):
        return jnp.where(i < self.lat_tiles, i // self.tiles_per_seq, self.B)

    def seq_pos(self, i):
        is_lat = i < self.lat_tiles
        seqlen = jnp.where(is_lat, self.T, self.C)
        row0 = i * self.tm - jnp.where(is_lat, 0, self.n_lat)
        return lax.rem(row0, seqlen), seqlen

    def halo_specs(self, width, n_rows, col=0):
        blocks = n_rows // HALO
        per = self.tm // HALO
        prev = pl.BlockSpec((HALO, width), lambda i: (jnp.maximum(i * per - 1, 0), col))
        nxt = pl.BlockSpec((HALO, width), lambda i: (jnp.minimum((i + 1) * per, blocks - 1), col))
        return prev, nxt


def _dwconv3_ext(xe, w, pos0, seqlen):
    n = TM + 2 * HALO
    pos = pos0 + lax.broadcasted_iota(jnp.int32, (TM, 1), 0)
    xm = pltpu.roll(xe, 1, 0)[HALO:HALO + TM]
    xc = xe[HALO:HALO + TM]
    xn = pltpu.roll(xe, n - 1, 0)[HALO:HALO + TM]
    xm = jnp.where(pos == 0, 0.0, xm)
    xn = jnp.where(pos == seqlen - 1, 0.0, xn)
    return xm * w[0:1] + xc * w[1:2] + xn * w[2:3]


def _mod_kernel(cc_ref, w_ref, b_ref, o_ref):
    x = cc_ref[...]
    a = (x * _sigmoid(x)).astype(BF16)
    o_ref[0] = jnp.dot(a, w_ref[0].astype(BF16), preferred_element_type=F32) + b_ref[0]


def _modulation(cc, ada_w, ada_b):
    depth = ada_w.shape[0]
    nb = N_MOD * D // 1024
    return pl.pallas_call(
        _mod_kernel,
        grid=(depth, nb),
        in_specs=[pl.BlockSpec((8, D), lambda l, j: (0, 0)),
                  pl.BlockSpec((1, D, 1024), lambda l, j: (l, 0, j)),
                  pl.BlockSpec((1, 1, 1024), lambda l, j: (l, 0, j))],
        out_specs=pl.BlockSpec((1, 8, 1024), lambda l, j: (l, 0, j)),
        out_shape=jax.ShapeDtypeStruct((depth, 8, N_MOD * D), F32),
        compiler_params=_cparams(("arbitrary", "arbitrary")),
        name="adaln_mod",
    )(cc, ada_w, ada_b.reshape(depth, 1, N_MOD * D))


def _split_rows_specs(rows, width):
    lat = pl.BlockSpec((rows.tm, width), lambda i: (jnp.minimum(i, rows.lat_tiles - 1), 0))
    ctx = pl.BlockSpec((rows.tm, width), lambda i: (jnp.maximum(i - rows.lat_tiles, 0), 0))
    return lat, ctx


def _pick_rows(rows, lat_ref, ctx_ref):
    return jnp.where(pl.program_id(0) < rows.lat_tiles, lat_ref[...], ctx_ref[...])


def _in_kernel(x_ref, c_ref, g_ref, mod_ref, w_ref, o_ref, *, rows):
    m = mod_ref[0]
    u = _norm_mod(_pick_rows(rows, x_ref, c_ref), g_ref[...], m[:, 0:D], m[:, D:2 * D]).astype(BF16)
    o_ref[...] = jnp.dot(u, w_ref[...], preferred_element_type=F32)


def _in_proj(rows, x_lat, x_ctx, g, mod3, layer, w_bf16, name):
    width = w_bf16.shape[1]
    lat_spec, ctx_spec = _split_rows_specs(rows, D)
    return pl.pallas_call(
        functools.partial(_in_kernel, rows=rows),
        grid=(rows.tot_tiles,),
        in_specs=[lat_spec, ctx_spec,
                  _resident((1, D)),
                  pl.BlockSpec((1, 1, N_MOD * D), lambda i: (layer * 8 + rows.mod_row(i), 0, 0)),
                  _resident((D, width))],
        out_specs=pl.BlockSpec((rows.tm, width), lambda i: (i, 0)),
        out_shape=jax.ShapeDtypeStruct((rows.n_tot, width), F32),
        compiler_params=_cparams(("arbitrary",)),
        name=name,
    )(x_lat, x_ctx, g, mod3, w_bf16)


def _fft_tables(T):
    n1 = T // FFT_N2
    c = np.arange(FOURIER_WIDTH)
    ang = 2.0 * np.pi * (np.outer(c, c) % FOURIER_WIDTH) / FOURIER_WIDTH
    fc = np.concatenate([np.cos(ang), -np.sin(ang)], axis=1) / np.sqrt(FOURIER_WIDTH)
    t1 = np.arange(n1)[None, None, :]
    k1 = np.arange(n1)[None, :, None]
    t2 = np.arange(FFT_N2)[:, None, None]
    ang = 2.0 * np.pi * (((FFT_N2 * t1 + t2) * k1) % T) / T
    cm, sm = np.cos(ang) / np.sqrt(n1), np.sin(ang) / np.sqrt(n1)
    m1 = np.concatenate([np.concatenate([cm, sm], axis=2),
                         np.concatenate([-sm, cm], axis=2)], axis=1)
    k2 = np.arange(FFT_N2)
    ang = 2.0 * np.pi * (np.outer(k2, k2) % FFT_N2) / FFT_N2
    cs2 = np.concatenate([np.cos(ang), np.sin(ang)], axis=1) / np.sqrt(FFT_N2)
    return (jnp.asarray(fc, F32), jnp.asarray(m1, F32), jnp.asarray(cs2, F32))


def _ctx_fft_table(C):
    t = np.arange(C)
    ang = 2.0 * np.pi * (np.outer(t, t) % C) / C
    return jnp.asarray(np.concatenate([np.cos(ang), np.sin(ang)], axis=1) / np.sqrt(C), F32)


def _fft1_kernel(z_ref, fc_ref, m_ref, y_ref):
    fc = fc_ref[...].astype(BF16)
    for j in range(FFT_SUB):
        a = jnp.dot(z_ref[:, j, :].astype(BF16), fc, preferred_element_type=F32)
        st = jnp.concatenate([a[:, :FOURIER_WIDTH], a[:, FOURIER_WIDTH:]], axis=0).astype(BF16)
        y_ref[0, j] = jnp.dot(m_ref[j].astype(BF16), st, preferred_element_type=F32)


def _fft2_kernel(yr_ref, yi_ref, cs_ref, o_ref):
    cs = cs_ref[...].astype(BF16)
    for j in range(FFT_SUB):
        st = jnp.concatenate([yr_ref[0, :, j, :], yi_ref[0, :, j, :]], axis=0).astype(BF16)
        o_ref[0, :, j, :] = jnp.dot(cs, st, preferred_element_type=F32)


def _ctx_fft_kernel(z_ref, fc_ref, ct_ref, o_ref):
    a = jnp.dot(z_ref[...].astype(BF16), fc_ref[...].astype(BF16), preferred_element_type=F32)
    st = jnp.concatenate([a[:, :FOURIER_WIDTH], a[:, FOURIER_WIDTH:]], axis=0).astype(BF16)
    o_ref[...] = jnp.dot(ct_ref[...].astype(BF16), st, preferred_element_type=F32)


def _fourier_spec(rows, z):
    B, T, C = rows.B, rows.T, rows.C
    n1 = T // FFT_N2
    fc, m1, cs2 = _fft_tables(T)
    zv = z.reshape(rows.n_tot // FFT_N2, FFT_N2, D)
    lane_blocks = D // FOURIER_WIDTH
    ybuf = pl.pallas_call(
        _fft1_kernel,
        grid=(B, FFT_N2 // FFT_SUB),
        in_specs=[pl.BlockSpec((n1, FFT_SUB, FOURIER_WIDTH), lambda b, j: (b, j, lane_blocks - 1)),
                  _resident((FOURIER_WIDTH, 2 * FOURIER_WIDTH)),
                  pl.BlockSpec((FFT_SUB, 2 * n1, 2 * n1), lambda b, j: (j, 0, 0))],
        out_specs=pl.BlockSpec((1, FFT_SUB, 2 * n1, FOURIER_WIDTH), lambda b, j: (b, j, 0, 0)),
        out_shape=jax.ShapeDtypeStruct((B, FFT_N2, 2 * n1, FOURIER_WIDTH), F32),
        compiler_params=_cparams(("arbitrary", "arbitrary")),
        name="fft_stage1",
    )(zv, fc, m1)
    nblk = n1 // FFT_SUB
    spec_lat = pl.pallas_call(
        _fft2_kernel,
        grid=(B, nblk),
        in_specs=[pl.BlockSpec((1, FFT_N2, FFT_SUB, FOURIER_WIDTH), lambda b, j: (b, 0, j, 0)),
                  pl.BlockSpec((1, FFT_N2, FFT_SUB, FOURIER_WIDTH), lambda b, j: (b, 0, j + nblk, 0)),
                  _resident((FFT_N2, 2 * FFT_N2))],
        out_specs=pl.BlockSpec((1, FFT_N2, FFT_SUB, FOURIER_WIDTH), lambda b, j: (b, 0, j, 0)),
        out_shape=jax.ShapeDtypeStruct((B, FFT_N2, n1, FOURIER_WIDTH), F32),
        compiler_params=_cparams(("arbitrary", "arbitrary")),
        name="fft_stage2",
    )(ybuf, ybuf, cs2)
    ct = _ctx_fft_table(C)
    spec_ctx = pl.pallas_call(
        _ctx_fft_kernel,
        grid=(B,),
        in_specs=[pl.BlockSpec((C, FOURIER_WIDTH), lambda b: (rows.n_lat // C + b, lane_blocks - 1)),
                  _resident((FOURIER_WIDTH, 2 * FOURIER_WIDTH)),
                  _resident((C, 2 * C))],
        out_specs=pl.BlockSpec((C, FOURIER_WIDTH), lambda b: (b, 0)),
        out_shape=jax.ShapeDtypeStruct((B * C, FOURIER_WIDTH), F32),
        compiler_params=_cparams(("arbitrary",)),
        name="fft_ctx",
    )(z, fc, ct)
    return spec_lat.reshape(rows.n_lat, FOURIER_WIDTH), spec_ctx


def _ab_out_kernel(z_ref, zp_ref, zn_ref, spec_ref, spec_c_ref, h_ref, h_c_ref, mod_ref, pw_ref, ps_ref,
                   fw_ref, wo_ref, o_ref, *, rows):
    i = pl.program_id(0)
    pos0, seqlen = rows.seq_pos(i)
    n = TM + 2 * HALO
    ze = jnp.concatenate([zp_ref[...], z_ref[...], zn_ref[...]], axis=0)
    pe = pos0 - HALO + lax.broadcasted_iota(jnp.int32, (n, 1), 0)
    ze = jnp.where((pe >= 0) & (pe < seqlen), ze, 0.0)
    sums = {1: ze}
    w = 1
    while w < max(POOL_WINDOWS):
        sums[2 * w] = sums[w] + pltpu.roll(sums[w], n - w, 0)
        w *= 2
    pos = pos0 + lax.broadcasted_iota(jnp.int32, (TM, 1), 0)
    lane = lax.broadcasted_iota(jnp.int32, (1, POOL_WIDTH), 1)
    zc = ze[HALO:HALO + TM]
    mean = jnp.zeros((TM, POOL_WIDTH), F32)
    for g, w in enumerate(POOL_WINDOWS):
        left = w // 2
        right = w - 1 - left
        cw = pltpu.roll(sums[w], left, 0)[HALO:HALO + TM]
        cnt = (jnp.minimum(pos + right + 1, seqlen) - jnp.maximum(pos - left, 0)).astype(F32)
        in_group = (lane >= g * POOL_GROUP) & (lane < (g + 1) * POOL_GROUP)
        mean = jnp.where(in_group, cw / cnt, mean)
    pooled = (mean - zc).astype(BF16)
    ya = jnp.dot(pooled, pw_ref[...], preferred_element_type=F32) * ps_ref[...]
    yb = jnp.dot(_pick_rows(rows, spec_ref, spec_c_ref).astype(BF16), fw_ref[...], preferred_element_type=F32)
    y = (jnp.dot(ya.astype(BF16), wo_ref[0:POOL_WIDTH, :], preferred_element_type=F32)
         + jnp.dot(yb.astype(BF16), wo_ref[POOL_WIDTH:, :], preferred_element_type=F32))
    gate = mod_ref[0][:, 2 * D:3 * D]
    o_ref[...] = _pick_rows(rows, h_ref, h_c_ref) + gate * y


def _ab_out(rows, z, spec_lat, spec_ctx, x_lat, x_ctx, mod3, layer, pw_bd, pscale, fw, wo):
    n_rows = rows.n_tot
    zprev, znext = rows.halo_specs(POOL_WIDTH, n_rows)
    return pl.pallas_call(
        functools.partial(_ab_out_kernel, rows=rows),
        grid=(n_rows // TM,),
        in_specs=[pl.BlockSpec((TM, POOL_WIDTH), lambda i: (i, 0)), zprev, znext,
                  *_split_rows_specs(rows, FOURIER_WIDTH), *_split_rows_specs(rows, D),
                  pl.BlockSpec((1, 1, N_MOD * D), lambda i: (layer * 8 + rows.mod_row(i), 0, 0)),
                  _resident((POOL_WIDTH, POOL_WIDTH)), _resident((1, POOL_WIDTH)),
                  _resident((FOURIER_WIDTH, FOURIER_WIDTH)), _resident((D, D))],
        out_specs=pl.BlockSpec((TM, D), lambda i: (i, 0)),
        out_shape=jax.ShapeDtypeStruct((n_rows, D), F32),
        compiler_params=_cparams(("arbitrary",)),
        name="ab_out",
    )(z, z, z, spec_lat, spec_ctx, x_lat, x_ctx, mod3, pw_bd, pscale, fw, wo)


def _ffn_kernel(h_ref, hp_ref, hn_ref, g_ref, mod_ref, wu_ref, cw_ref, wd_ref, o_ref, up_ref, act_ref, *, rows):
    i = pl.program_id(0)
    pos0, seqlen = rows.seq_pos(i)
    tm = rows.tm
    n = tm + 2 * HALO
    m = mod_ref[0]
    he = jnp.concatenate([hp_ref[...], h_ref[...], hn_ref[...]], axis=0)
    u = _norm_mod(he, g_ref[...], m[:, 3 * D:4 * D], m[:, 4 * D:5 * D])
    pe = pos0 - HALO + lax.broadcasted_iota(jnp.int32, (n, 1), 0)
    u = jnp.where((pe >= 0) & (pe < seqlen), u, 0.0).astype(BF16)

    def conv(slot, col):
        w = cw_ref[:, col:col + FF_CHUNK]
        return (up_ref[slot, HALO - 1:HALO - 1 + tm, :] * w[0:1] + up_ref[slot, HALO:HALO + tm, :] * w[1:2]
                + up_ref[slot, HALO + 1:HALO + 1 + tm, :] * w[2:3])

    for c in range(D_FF // FF_CHUNK):
        lo = c * FF_CHUNK
        sg, sv = 2 * (c % 2), 2 * (c % 2) + 1
        up_ref[sg] = jnp.dot(u, wu_ref[:, lo:lo + FF_CHUNK], preferred_element_type=F32)
        up_ref[sv] = jnp.dot(u, wu_ref[:, D_FF + lo:D_FF + lo + FF_CHUNK], preferred_element_type=F32)
        gate = conv(sg, lo)
        val = conv(sv, D_FF + lo)
        half = 0.5 * gate
        act_ref[:, lo:lo + FF_CHUNK] = ((half + half * jnp.tanh(half)) * val).astype(BF16)
    acc = jnp.dot(act_ref[...], wd_ref[...], preferred_element_type=F32)
    o_ref[...] = h_ref[...] + m[:, 5 * D:6 * D] * acc


def _ffn(rows, h, g, mod3, layer, wu, cw, wd):
    n_rows = h.shape[0]
    tm = rows.tm
    hprev, hnext = rows.halo_specs(D, n_rows)
    return pl.pallas_call(
        functools.partial(_ffn_kernel, rows=rows),
        grid=(n_rows // tm,),
        in_specs=[pl.BlockSpec((tm, D), lambda i: (i, 0)), hprev, hnext,
                  _resident((1, D)),
                  pl.BlockSpec((1, 1, N_MOD * D), lambda i: (layer * 8 + rows.mod_row(i), 0, 0)),
                  _resident((D, 2 * D_FF)), _resident((3, 2 * D_FF)), _resident((D_FF, D))],
        out_specs=pl.BlockSpec((tm, D), lambda i: (i, 0)),
        out_shape=jax.ShapeDtypeStruct((n_rows, D), F32),
        scratch_shapes=[pltpu.VMEM((4, tm + 2 * HALO, FF_CHUNK), F32), pltpu.VMEM((tm, D_FF), BF16)],
        compiler_params=_cparams(("arbitrary",)),
        name="conv_ffn",
    )(h, h, h, g, mod3, wu, cw, wd)


def _rope_tables(T, tm):
    half = HEAD_DIM // 2
    inv = ROPE_THETA ** (-np.arange(0, half, 2, dtype=np.float64) / half)
    t = np.arange(T)
    ang_r = (t // GRID_W)[:, None] * inv[None, :]
    ang_c = (t % GRID_W)[:, None] * inv[None, :]
    cos = np.concatenate([np.cos(ang_r), np.cos(ang_r), np.cos(ang_c), np.cos(ang_c)], axis=1)
    sin = np.concatenate([-np.sin(ang_r), np.sin(ang_r), -np.sin(ang_c), np.sin(ang_c)], axis=1)
    cos = np.concatenate([np.tile(cos, (1, 2)), np.ones((tm, 128))], axis=0)
    sin = np.concatenate([np.tile(sin, (1, 2)), np.zeros((tm, 128))], axis=0)
    return jnp.asarray(cos, F32), jnp.asarray(sin, F32)


def _rope128(x, cos, sin):
    lane = lax.broadcasted_iota(jnp.int32, (1, 128), 1)
    partner = jnp.where((lane & 31) < 16, pltpu.roll(x, 128 - 16, 1), pltpu.roll(x, 16, 1))
    return x * cos + partner * sin


def _cd_in_kernel(x_ref, g_ref, mod_ref, w_ref, cos_ref, sin_ref, qn_ref, kn_ref, ones_ref,
                  q_out, k_out, v_out, rkv_out, lora_out):
    m = mod_ref[0]
    u = _norm_mod(x_ref[...], g_ref[...], m[:, 0:D], m[:, D:2 * D]).astype(BF16)
    n_att, n_rkv, n_lora = CD_SPLITS
    rkv_out[...] = jnp.dot(u, w_ref[:, n_att:n_att + n_rkv], preferred_element_type=F32)
    lora_out[...] = jnp.dot(u, w_ref[:, n_att + n_rkv:], preferred_element_type=F32)
    p = jnp.dot(u, w_ref[:, 0:n_att], preferred_element_type=F32)
    cos, sin = cos_ref[...], sin_ref[...]
    q = p[:, 0:ATT_WIDTH]
    ms = _seg_sum(q * q, ones_ref[...]) * (1.0 / HEAD_DIM)
    q = q * lax.rsqrt(ms + EPS) * qn_ref[...]
    q = jnp.concatenate([_rope128(q[:, 128 * j:128 * (j + 1)], cos, sin) for j in range(4)], axis=1)
    q_out[...] = (q * HEAD_DIM ** -0.5).astype(BF16)
    k = p[:, ATT_WIDTH:ATT_WIDTH + KV_WIDTH]
    ms = _seg_sum(k * k, ones_ref[0:KV_WIDTH, 0:KV_WIDTH]) * (1.0 / HEAD_DIM)
    k = k * lax.rsqrt(ms + EPS) * kn_ref[...]
    k_out[...] = _rope128(k, cos, sin).astype(BF16)
    v_out[...] = p[:, ATT_WIDTH + KV_WIDTH:].astype(BF16)


def _cd_in(rows, h, g, mod3, layer, w_bf16, q_norm, k_norm, ones_bd):
    cos, sin = _rope_tables(rows.T, rows.tm)
    tps = rows.tiles_per_seq
    tab = lambda i: (jnp.where(i < rows.lat_tiles, lax.rem(i, tps), tps), 0)
    qn = jnp.tile(q_norm, N_Q_HEADS).reshape(1, ATT_WIDTH)
    kn = jnp.tile(k_norm, N_KV_HEADS).reshape(1, KV_WIDTH)
    widths = (ATT_WIDTH, KV_WIDTH, KV_WIDTH, CD_SPLITS[1], CD_SPLITS[2])
    dtypes = (BF16, BF16, BF16, F32, F32)
    return pl.pallas_call(
        _cd_in_kernel,
        grid=(rows.tot_tiles,),
        in_specs=[pl.BlockSpec((rows.tm, D), lambda i: (i, 0)),
                  _resident((1, D)),
                  pl.BlockSpec((1, 1, N_MOD * D), lambda i: (layer * 8 + rows.mod_row(i), 0, 0)),
                  _resident((D, sum(CD_SPLITS))),
                  pl.BlockSpec((rows.tm, 128), tab), pl.BlockSpec((rows.tm, 128), tab),
                  _resident((1, ATT_WIDTH)), _resident((1, KV_WIDTH)),
                  _resident((RWKV_WIDTH, RWKV_WIDTH))],
        out_specs=[pl.BlockSpec((rows.tm, n), lambda i: (i, 0)) for n in widths],
        out_shape=[jax.ShapeDtypeStruct((rows.n_tot, n), dt) for n, dt in zip(widths, dtypes)],
        compiler_params=_cparams(("arbitrary",)),
        name="cd_in",
    )(h, g, mod3, w_bf16, cos, sin, qn, kn, ones_bd)


def _attn_bias(C):
    Q = ATT_BLOCK
    ql = (np.arange(GQA_GROUP * Q) % Q)[:, None]
    col = np.arange(3 * Q + C)[None, :]
    rel = col - ql
    band = (rel >= 0) & (rel <= 2 * Q) & (col < 3 * Q)
    is_ctx = np.broadcast_to(col >= 3 * Q, band.shape)
    variants = [band & (col >= Q) | is_ctx, band | is_ctx, band & (col < 2 * Q) | is_ctx]
    return jnp.asarray(np.where(np.stack(variants), 0.0, -1e30), F32)


def _attn_kernel(q_ref, kp_ref, kc_ref, kn_ref, kx_ref, vp_ref, vc_ref, vn_ref, vx_ref, sink_ref, bias_ref,
                 o_ref, *, nq):
    i = pl.program_id(1)
    Q = ATT_BLOCK
    heads = range(N_KV_HEADS)
    k_all = jnp.concatenate([kp_ref[...], kc_ref[...], kn_ref[...], kx_ref[...]], axis=0)
    v_all = jnp.concatenate([vp_ref[...], vc_ref[...], vn_ref[...], vx_ref[...]], axis=0)
    bias = bias_ref[jnp.where(i == 0, 0, jnp.where(i == nq - 1, 2, 1))]
    q = q_ref[...]
    q4 = [jnp.concatenate([q[:, HEAD_DIM * (GQA_GROUP * hk + g):HEAD_DIM * (GQA_GROUP * hk + g + 1)]
                           for g in range(GQA_GROUP)], axis=0) for hk in heads]
    kh = [k_all[:, HEAD_DIM * hk:HEAD_DIM * (hk + 1)] for hk in heads]
    vh = [v_all[:, HEAD_DIM * hk:HEAD_DIM * (hk + 1)] for hk in heads]
    s = [lax.dot_general(q4[hk], kh[hk], (((1,), (1,)), ((), ())), preferred_element_type=F32) + bias
         for hk in heads]
    sk = [sink_ref[hk][:, 0:1] for hk in heads]
    m = [jnp.maximum(jnp.max(s[hk], axis=1, keepdims=True), sk[hk]) for hk in heads]
    p = [jnp.exp(s[hk] - m[hk]) for hk in heads]
    den = [jnp.sum(p[hk], axis=1, keepdims=True) + jnp.exp(sk[hk] - m[hk]) for hk in heads]
    o = [jnp.dot(p[hk].astype(BF16), vh[hk], preferred_element_type=F32) * (1.0 / den[hk]) for hk in heads]
    o_ref[...] = jnp.concatenate([o[hk][Q * g:Q * (g + 1)] for hk in heads for g in range(GQA_GROUP)],
                                 axis=1).astype(BF16)


def _attention(rows, qr, kr, vr, sink):
    B, T, C = rows.B, rows.T, rows.C
    nq = T // ATT_BLOCK
    assert nq >= 2
    blk = lambda off: pl.BlockSpec(
        (ATT_BLOCK, KV_WIDTH), lambda b, i: (b * nq + jnp.clip(i + off, 0, nq - 1), 0))
    ctx = pl.BlockSpec((C, KV_WIDTH), lambda b, i: (rows.n_lat // C + b, 0))
    sink_col = jnp.repeat(sink.reshape(N_KV_HEADS, GQA_GROUP), ATT_BLOCK, axis=1)
    sink_col = jnp.broadcast_to(sink_col[:, :, None], (N_KV_HEADS, GQA_GROUP * ATT_BLOCK, 128))
    return pl.pallas_call(
        functools.partial(_attn_kernel, nq=nq),
        grid=(B, nq),
        in_specs=[pl.BlockSpec((ATT_BLOCK, ATT_WIDTH), lambda b, i: (b * nq + i, 0)),
                  blk(-1), blk(0), blk(1), ctx, blk(-1), blk(0), blk(1), ctx,
                  _resident((N_KV_HEADS, GQA_GROUP * ATT_BLOCK, 128)),
                  _resident((3, GQA_GROUP * ATT_BLOCK, 3 * ATT_BLOCK + C))],
        out_specs=pl.BlockSpec((ATT_BLOCK, ATT_WIDTH), lambda b, i: (b * nq + i, 0)),
        out_shape=jax.ShapeDtypeStruct((rows.n_lat, ATT_WIDTH), BF16),
        compiler_params=_cparams(("arbitrary", "arbitrary")),
        name="window_attention",
    )(qr, kr, kr, kr, kr, vr, vr, vr, vr, sink_col, _attn_bias(C))


def _rwkv_prep_kernel(x_ref, xp_ref, xn_ref, lora_ref, cw_ref, w0_ref, w2_ref, a0_ref, a2_ref,
                      kk_ref, ka_ref, rk_ref, ones_ref,
                      r_out, v_out, kkn_out, w_out0, kka_out0, km_out0, w_out1, kka_out1, km_out1, bonus_out,
                      *, rows):
    i = pl.program_id(0)
    pos0, seqlen = rows.seq_pos(i)
    xe = jnp.concatenate([xp_ref[...], x_ref[...], xn_ref[...]], axis=0)
    rkv = _dwconv3_ext(xe, cw_ref[...], pos0, seqlen)
    W = RWKV_WIDTH
    r, k, v = rkv[:, 0:W], rkv[:, W:2 * W], rkv[:, 2 * W:3 * W]
    kk = k * kk_ref[...]
    kk = kk * lax.rsqrt(_seg_sum(kk * kk, ones_ref[...]) + EPS)
    r_out[...] = r
    v_out[...] = v
    kkn_out[...] = kk
    lora = lora_ref[...]
    wl = jnp.tanh(lora[:, 0:64]).astype(BF16)
    al = lora[:, 64:128].astype(BF16)
    outs = ((w_out0, kka_out0, km_out0), (w_out1, kka_out1, km_out1))
    bonus = jnp.zeros_like(r)
    for d in range(2):
        x = -(w0_ref[d:d + 1, :] + jnp.dot(wl, w2_ref[d], preferred_element_type=F32))
        softplus = jnp.maximum(x, 0.0) + jnp.log(1.0 + jnp.exp(-jnp.abs(x)))
        w_log = -softplus - 0.5
        a = _sigmoid(a0_ref[d:d + 1, :] + jnp.dot(al, a2_ref[d], preferred_element_type=F32))
        w_o, kka_o, km_o = outs[d]
        w_o[...] = -jnp.exp(w_log)
        kka_o[...] = kk * a
        km = k * (1.0 + (a - 1.0) * ka_ref[...])
        km_o[...] = km
        bonus = bonus + km
    bonus_out[...] = _seg_sum(r * bonus * rk_ref[...], ones_ref[...]) * v


def _rwkv_prep(rows, p_rkv, p_lora, conv_w, w0, w2, a0, a2, k_k, k_a, r_k, ones_bd):
    n_rows = rows.n_tot
    W = RWKV_WIDTH
    xprev, xnext = rows.halo_specs(3 * W, n_rows)
    out_spec = pl.BlockSpec((TM, W), lambda i: (i, 0))
    return pl.pallas_call(
        functools.partial(_rwkv_prep_kernel, rows=rows),
        grid=(rows.tot_tiles,),
        in_specs=[pl.BlockSpec((TM, 3 * W), lambda i: (i, 0)), xprev, xnext,
                  pl.BlockSpec((TM, CD_SPLITS[2]), lambda i: (i, 0)),
                  _resident((3, 3 * W)), _resident((2, W)), _resident((2, 64, W)),
                  _resident((2, W)), _resident((2, 64, W)), _resident((1, W)), _resident((1, W)),
                  _resident((1, W)), _resident((W, W))],
        out_specs=[out_spec] * 10,
        out_shape=[jax.ShapeDtypeStruct((n_rows, W), F32)] * 10,
        compiler_params=_cparams(("arbitrary",)),
        name="rwkv_prep",
    )(p_rkv, p_rkv, p_rkv, p_lora, conv_w, w0, w2, a0, a2, k_k, k_a, r_k, ones_bd)


def _bd(x, bd_mask):
    xb = x.astype(BF16)
    return jnp.where(bd_mask, jnp.concatenate([xb] * SCAN_HEADS, axis=0), jnp.zeros((), BF16))


def _mm(a, b):
    return jnp.dot(a.astype(BF16), b, preferred_element_type=F32)


def _diag_blocks(f):
    lane_head = lax.broadcasted_iota(jnp.int32, (SCAN_L, 256), 1) >> 6
    out = jnp.zeros((SCAN_L, 256), F32)
    for h in range(SCAN_HEADS):
        out = jnp.where(lane_head == h, f[SCAN_L * h:SCAN_L * (h + 1)], out)
    return out


def _chunk_step(chains, states, dirs):
    L = SCAN_L
    n = len(chains)
    row = lax.broadcasted_iota(jnp.int32, (L, 256), 0)
    idx = lax.broadcasted_iota(jnp.int32, (L, 256), 1) & (L - 1)
    incl = {True: idx >= row, False: idx <= row}
    strict = {True: idx > row, False: idx < row}
    eye = (idx == row).astype(F32)
    bd_mask = ((lax.broadcasted_iota(jnp.int32, (256, 256), 0) >> 6)
               == (lax.broadcasted_iota(jnp.int32, (256, 256), 1) >> 6))
    r64 = lax.broadcasted_iota(jnp.int32, (L, L), 0)
    c64 = lax.broadcasted_iota(jnp.int32, (L, L), 1)
    tri = {True: (c64 >= r64).astype(BF16), False: (c64 <= r64).astype(BF16)}
    nt = (((1,), (1,)), ((), ()))
    tn = (((0,), (0,)), ((), ()))
    bd = lambda x: _bd(x, bd_mask)
    last = {True: 0, False: L - 1}

    def cum_log_decay(lw, tri):
        h1 = lw.astype(BF16)
        r1 = lw - h1.astype(F32)
        h2 = r1.astype(BF16)
        h3 = (r1 - h2.astype(F32)).astype(BF16)
        return (jnp.dot(tri, h1, preferred_element_type=F32) + jnp.dot(tri, h2, preferred_element_type=F32)
                + jnp.dot(tri, h3, preferred_element_type=F32))

    lcs = [cum_log_decay(c[3], tri[d]) for c, d in zip(chains, dirs)]
    pre = []
    for (r, v, kk, lw, kka, km), lc, d in zip(chains, lcs, dirs):
        e_in = jnp.exp(lc)
        e_neg = jnp.exp(-lc)
        p_end = e_in[last[d]:last[d] + 1]
        bb = kka * e_neg
        kb = km * e_neg
        pre.append(dict(ab=-kk * jnp.exp(lc - lw), bb=bb, kb=kb, rb=r * e_in, bh=bb * p_end, kh=kb * p_end,
                        p_end=p_end, v=v))
    lhs = [jnp.concatenate([p["ab"], p["rb"]], axis=0).astype(BF16) for p in pre]
    g1 = [lax.dot_general(l, bd(p["bb"]), nt, preferred_element_type=F32) for l, p in zip(lhs, pre)]
    g2 = [lax.dot_general(l, bd(p["kb"]), nt, preferred_element_type=F32) for l, p in zip(lhs, pre)]
    aab = [jnp.where(strict[d], g[:L], 0.0) for g, d in zip(g1, dirs)]
    brb = [jnp.where(incl[d], g[L:], 0.0) for g, d in zip(g1, dirs)]
    aak = [jnp.where(strict[d], g[:L], 0.0) for g, d in zip(g2, dirs)]
    brk = [jnp.where(incl[d], g[L:], 0.0) for g, d in zip(g2, dirs)]

    def coupling(m):
        sh = m.bit_length() - 1
        tb, sb = row >> sh, idx >> sh
        same = (row >> (sh + 1)) == (idx >> (sh + 1))
        return {True: same & ((sb & 1) == 1) & ((tb & 1) == 0), False: same & ((tb & 1) == 1) & ((sb & 1) == 0)}

    cm = coupling(1)
    tw = [eye + jnp.where(cm[d], a, 0.0) for a, d in zip(aab, dirs)]
    m = 2
    while m < L:
        cm = coupling(m)
        x = [_mm(t, bd(jnp.where(cm[d], a, 0.0))) for t, a, d in zip(tw, aab, dirs)]
        tw = [t + _mm(xi, bd(t)) for t, xi in zip(tw, x)]
        m *= 2

    v_bd = [bd(p["v"]) for p in pre]
    av = [_mm(a, vb) for a, vb in zip(aak, v_bd)]
    wu = [_mm(t, jnp.concatenate([bd(p["ab"]), bd(a)], axis=1)) for t, p, a in zip(tw, pre, av)]
    qy = [_mm(b, jnp.concatenate([bd(w[:, :256]), bd(w[:, 256:])], axis=1)) for b, w in zip(brb, wu)]
    y2b = [_mm(b, vb) for b, vb in zip(brk, v_bd)]
    f1 = [lax.dot_general(p["bh"].astype(BF16), w.astype(BF16), tn, preferred_element_type=F32)
          for p, w in zip(pre, wu)]
    f2 = [lax.dot_general(p["kh"].astype(BF16), p["v"].astype(BF16), tn, preferred_element_type=F32)
          for p in pre]
    ys, new_states = [], []
    for i in range(n):
        q = pre[i]["rb"] + qy[i][:, :256]
        y2 = qy[i][:, 256:] + y2b[i]
        mw = eye * pre[i]["p_end"] + _diag_blocks(f1[i][:, :256])
        nw = _diag_blocks(f1[i][:, 256:]) + _diag_blocks(f2[i])
        qm = jnp.concatenate([q, mw], axis=0).astype(BF16)
        z = jnp.dot(qm, bd(states[i]), preferred_element_type=F32)
        ys.append(z[:L] + y2)
        new_states.append(z[L:] + nw)
    return ys, new_states


def _chunk_scan_kernel(*refs, n_batch):
    n_dir = 2
    ins = [refs[6 * j:6 * (j + 1)] for j in range(n_dir * n_batch)]
    y_refs = refs[6 * n_dir * n_batch:6 * n_dir * n_batch + n_dir]
    s_ref = refs[6 * n_dir * n_batch + n_dir]

    @pl.when(pl.program_id(0) == 0)
    def _():
        s_ref[...] = jnp.zeros_like(s_ref)

    groups = [(d, b, slice(256 * g, 256 * (g + 1))) for d in range(n_dir) for b in range(n_batch)
              for g in range(RWKV_HEADS // SCAN_HEADS)]
    chains = [tuple(ref[:, sl] for ref in ins[d * n_batch + b]) for d, b, sl in groups]
    states = [s_ref[d, b, :, sl] for d, b, sl in groups]
    ys, new_states = _chunk_step(chains, states, [d == 1 for d, _, _ in groups])
    for (d, b, sl), y, s_new in zip(groups, ys, new_states):
        y_refs[d][b, :, sl] = y
        s_ref[d, b, :, sl] = s_new


def _rwkv_scan(rows, r, v, kk, per_dir):
    B, T, C = rows.B, rows.T, rows.C
    L = SCAN_L
    nctx, nlat = C // L, T // L

    def in_blk(c, b, reverse):
        j_ctx = (nctx - 1 - c) if reverse else c
        j_lat = (nlat - 1 - (c - nctx)) if reverse else (c - nctx)
        return (jnp.where(c < nctx, rows.n_lat // L + b * nctx + j_ctx, b * nlat + j_lat), 0)

    def out_blk(c, reverse):
        j = jnp.maximum(c - nctx, 0)
        return (0, (nlat - 1 - j) if reverse else j, 0)

    in_specs, args = [], []
    for reverse, (lw, kka, km) in zip((False, True), per_dir):
        for b in range(B):
            in_specs += [pl.BlockSpec((L, RWKV_WIDTH), functools.partial(in_blk, b=b, reverse=reverse))] * 6
            args += [r, v, kk, lw, kka, km]
    ys = pl.pallas_call(
        functools.partial(_chunk_scan_kernel, n_batch=B),
        grid=(nctx + nlat,),
        in_specs=in_specs,
        out_specs=[pl.BlockSpec((B, L, RWKV_WIDTH), functools.partial(out_blk, reverse=rv)) for rv in (False, True)],
        out_shape=[jax.ShapeDtypeStruct((B, T, RWKV_WIDTH), F32)] * 2,
        scratch_shapes=[pltpu.VMEM((2, B, HEAD_DIM, RWKV_WIDTH), F32)],
        compiler_params=_cparams(("arbitrary",)),
        name="rwkv_scan",
    )(*args)
    return [y.reshape(rows.n_lat, RWKV_WIDTH) for y in ys]


def _cd_out_kernel(y0_ref, y1_ref, bonus_ref, lora_ref, att_ref, h_ref, mod_ref,
                   lw_ref, lb_ref, gu_ref, wo_ref, ones_ref, o_ref):
    ones_bd = ones_ref[...]
    inv = 1.0 / HEAD_DIM
    y = y0_ref[...] + y1_ref[...]
    mu = _seg_sum(y, ones_bd) * inv
    yc = y - mu
    var = _seg_sum(yc * yc, ones_bd) * inv
    yn = yc * lax.rsqrt(var + GN_EPS) * lw_ref[...] + lb_ref[...]
    gate = jnp.dot(_sigmoid(lora_ref[...][:, 128:256]).astype(BF16), gu_ref[...], preferred_element_type=F32)
    mix = ((yn + bonus_ref[...]) * gate).astype(BF16)
    out = (jnp.dot(att_ref[...], wo_ref[0:ATT_WIDTH, :], preferred_element_type=F32)
           + jnp.dot(mix, wo_ref[ATT_WIDTH:, :], preferred_element_type=F32))
    o_ref[...] = h_ref[...] + mod_ref[0][:, 2 * D:3 * D] * out


def _cd_out(rows, y0, y1, bonus, p_lora, att, h, mod3, layer, lnx_w, lnx_b, g_up, wo, ones_bd):
    W = RWKV_WIDTH
    tile = lambda n: pl.BlockSpec((rows.tm, n), lambda i: (i, 0))
    return pl.pallas_call(
        _cd_out_kernel,
        grid=(rows.lat_tiles,),
        in_specs=[tile(W)] * 3 + [tile(CD_SPLITS[2]), tile(ATT_WIDTH), tile(D),
                  pl.BlockSpec((1, 1, N_MOD * D), lambda i: (layer * 8 + rows.mod_row(i), 0, 0)),
                  _resident((1, W)), _resident((1, W)),
                  _resident((128, W)), _resident((D, D)), _resident((W, W))],
        out_specs=tile(D),
        out_shape=jax.ShapeDtypeStruct((rows.n_lat, D), F32),
        compiler_params=_cparams(("arbitrary",)),
        name="cd_out",
    )(y0, y1, bonus, p_lora, att, h, mod3, lnx_w, lnx_b, g_up, wo, ones_bd)


def _block_diag(blocks):
    n = blocks.shape[0]
    g = blocks.shape[1]
    out = jnp.zeros((n * g, n * g), blocks.dtype)
    for j in range(n):
        out = out.at[j * g:(j + 1) * g, j * g:(j + 1) * g].set(blocks[j])
    return out


def kernel(x, c, ctx, c_ctx, ada_w, ada_b, norm1, norm2, ffn_up, ffn_conv, ffn_down, ab_w_in, pool_w,
           pool_scale, fourier_w, ab_w_out, cd_w_in, q_norm, k_norm, attn_sink, rwkv_conv, rwkv_w0,
           rwkv_w2, rwkv_a0, rwkv_a2, rwkv_k_k, rwkv_k_a, rwkv_r_k, rwkv_lnx_w, rwkv_lnx_b,
           rwkv_g_up, cd_w_out):
    B, T, _ = x.shape
    C = ctx.shape[1]
    depth = ada_w.shape[0]
    assert depth == 2 and B + 1 <= 8
    rows = _Rows(B, T, C)
    wide = _Rows(B, T, C, TM_WIDE)
    W = RWKV_WIDTH

    cc = jnp.zeros((8, D), F32).at[:B].set(c).at[B].set(c_ctx)
    mod3 = _modulation(cc, ada_w, ada_b).reshape(depth * 8, 1, N_MOD * D)
    ones_bd = _block_diag(jnp.ones((RWKV_HEADS, HEAD_DIM, HEAD_DIM), BF16))

    x_lat, x_ctx = x.reshape(B * T, D), ctx.reshape(B * C, D)

    z = _in_proj(wide, x_lat, x_ctx, norm1[0].reshape(1, D), mod3, 0, ab_w_in[0].astype(BF16), "ab_in")
    spec_lat, spec_ctx = _fourier_spec(rows, z)
    h = _ab_out(rows, z, spec_lat, spec_ctx, x_lat, x_ctx, mod3, 0, _block_diag(pool_w[0]).astype(BF16),
                pool_scale[0].reshape(1, POOL_WIDTH), fourier_w[0].astype(BF16), ab_w_out[0].astype(BF16))
    h = _ffn(rows, h, norm2[0].reshape(1, D), mod3, 0, ffn_up[0].astype(BF16), ffn_conv[0],
             ffn_down[0].astype(BF16))

    qr, kr, vr, p_rkv, p_lora = _cd_in(wide, h, norm1[1].reshape(1, D), mod3, 1, cd_w_in[0].astype(BF16),
                                       q_norm[0], k_norm[0], ones_bd)
    att = _attention(rows, qr, kr, vr, attn_sink[0])
    r, v, kk, w0, kka0, km0, w1, kka1, km1, bonus = _rwkv_prep(
        rows, p_rkv, p_lora, rwkv_conv[0], rwkv_w0[0], rwkv_w2[0].astype(BF16), rwkv_a0[0],
        rwkv_a2[0].astype(BF16), rwkv_k_k[0].reshape(1, W), rwkv_k_a[0].reshape(1, W),
        rwkv_r_k[0].reshape(1, W), ones_bd)
    y0, y1 = _rwkv_scan(rows, r, v, kk, ((w0, kka0, km0), (w1, kka1, km1)))
    h_lat = _cd_out(wide, y0, y1, bonus, p_lora, att, h, mod3, 1,
                    rwkv_lnx_w[0].reshape(1, W), rwkv_lnx_b[0].reshape(1, W),
                    rwkv_g_up[0].astype(BF16), cd_w_out[0].astype(BF16), ones_bd)
    lat_rows = _Rows(B, T, C, TM_WIDE)
    h_lat = _ffn(lat_rows, h_lat, norm2[1].reshape(1, D), mod3, 1, ffn_up[1].astype(BF16), ffn_conv[1],
                 ffn_down[1].astype(BF16))
    return h_lat.reshape(B, T, D)
```
